```python
import math
import jax, jax.numpy as jnp
from jax import lax
import numpy as np

D_MODEL = 1024
BATCH = 8
SEQ = 2048
DEPTH = 1
DEC_BATCH = 16
DEC_SEQ = 4096
PAST_LEN = 128

MIX_DIM = D_MODEL
POOL_WINDOWS = (2, 4, 8, 16)
N_POOL_GROUPS = len(POOL_WINDOWS)
POOL_DIM = MIX_DIM // 2
POOL_GROUP_DIM = POOL_DIM // N_POOL_GROUPS
N_HEADS = 8
QK_NOPE_DIM = 64
QK_ROPE_DIM = 32
V_HEAD_DIM = 64
ATTN_DIM = N_HEADS * V_HEAD_DIM
Q_LORA_RANK = 384
KV_LORA_RANK = 256
ROPE_THETA = 10000.0
Q_BLOCK = 128
IN_PROJ_DIM = POOL_DIM + Q_LORA_RANK + KV_LORA_RANK + QK_ROPE_DIM
N_EXPERTS = 16
CAPACITY_FACTOR = 2
FFN_HIDDEN = 512
LN_EPS = 1e-5
RMS_EPS = 1e-6
DEEPNORM_ALPHA = (2.0 * DEPTH) ** 0.25
DEEPNORM_BETA = (8.0 * DEPTH) ** -0.25

kernel_name = "hybrid_pool_mla_expert_choice_encoder"


def _layer_norm(x, g, b):
    xf = x.astype(jnp.float32)
    mu = jnp.mean(xf, axis=-1, keepdims=True)
    var = jnp.mean(jnp.square(xf - mu), axis=-1, keepdims=True)
    y = (xf - mu) * lax.rsqrt(var + LN_EPS) * g.astype(jnp.float32) + b.astype(jnp.float32)
    return y.astype(x.dtype)


def _rms_norm(x, g):
    xf = x.astype(jnp.float32)
    y = xf * lax.rsqrt(jnp.mean(jnp.square(xf), axis=-1, keepdims=True) + RMS_EPS) * g.astype(jnp.float32)
    return y.astype(x.dtype)


def _rope(x):
    s = x.shape[1]
    inv = 1.0 / (ROPE_THETA ** (jnp.arange(0, QK_ROPE_DIM, 2, dtype=jnp.float32) / QK_ROPE_DIM))
    ang = jnp.arange(s, dtype=jnp.float32)[:, None] * inv[None, :]
    cos = jnp.cos(ang)[None, :, None, :]
    sin = jnp.sin(ang)[None, :, None, :]
    x1, x2 = jnp.split(x.astype(jnp.float32), 2, axis=-1)
    return jnp.concatenate([x1 * cos - x2 * sin, x1 * sin + x2 * cos], axis=-1).astype(x.dtype)


def _pool_mixer(u, w_pool, pool_scale):
    b, s, _ = u.shape
    ug = u.reshape(b, s, N_POOL_GROUPS, POOL_GROUP_DIM).astype(jnp.float32)
    cs = jnp.concatenate([jnp.zeros((b, 1, N_POOL_GROUPS, POOL_GROUP_DIM), jnp.float32),
                          lax.cumsum(ug, axis=1)], axis=1)
    pos = jnp.arange(s)
    outs = []
    for g, w in enumerate(POOL_WINDOWS):
        lo = jnp.clip(pos - w // 2, 0, s - 1)
        hi = jnp.clip(pos + w // 2 - 1, 0, s - 1)
        win_sum = cs[:, hi + 1, g] - cs[:, lo, g]
        cnt = (hi - lo + 1).astype(jnp.float32)[None, :, None]
        outs.append(win_sum / cnt - ug[:, :, g])
    d = jnp.stack(outs, axis=2).astype(u.dtype)
    y = jnp.einsum('bsgc,gcd->bsgd', d, w_pool)
    return y.reshape(b, s, POOL_DIM) * pool_scale


def _mla(q_lat, kv_lat, k_rope_raw, q_norm_g, w_uq, kv_norm_g, w_ukv):
    b, s, _ = q_lat.shape
    q = jnp.einsum('bsr,rf->bsf', _rms_norm(q_lat, q_norm_g), w_uq).reshape(b, s, N_HEADS, QK_NOPE_DIM + QK_ROPE_DIM)
    q_nope = q[..., :QK_NOPE_DIM]
    q_rope = _rope(q[..., QK_NOPE_DIM:])
    kv = jnp.einsum('bsr,rf->bsf', _rms_norm(kv_lat, kv_norm_g), w_ukv).reshape(b, s, N_HEADS, QK_NOPE_DIM + V_HEAD_DIM)
    k_nope = kv[..., :QK_NOPE_DIM]
    v = kv[..., QK_NOPE_DIM:]
    k_rope = _rope(k_rope_raw[:, :, None, :])[:, :, 0, :]
    scale = 1.0 / math.sqrt(QK_NOPE_DIM + QK_ROPE_DIM)
    nq = s // Q_BLOCK
    qn_blk = q_nope.reshape(b, nq, Q_BLOCK, N_HEADS, QK_NOPE_DIM).swapaxes(0, 1)
    qr_blk = q_rope.reshape(b, nq, Q_BLOCK, N_HEADS, QK_ROPE_DIM).swapaxes(0, 1)

    def block(args):
        qn, qr = args
        sc = (jnp.einsum('bqhd,bkhd->bhqk', qn, k_nope, preferred_element_type=jnp.float32)
              + jnp.einsum('bqhd,bkd->bhqk', qr, k_rope, preferred_element_type=jnp.float32)) * scale
        p = jax.nn.softmax(sc, axis=-1)
        return jnp.einsum('bhqk,bkhd->bqhd', p.astype(v.dtype), v)

    o = lax.map(block, (qn_blk, qr_blk))
    return o.swapaxes(0, 1).reshape(b, s, ATTN_DIM)


def _expert_choice_ffn(x, w_router, w_gate, w_up, w_down):
    b, s, d = x.shape
    t = b * s
    cap = max(1, (CAPACITY_FACTOR * t) // N_EXPERTS)
    xt = x.reshape(t, d)
    aff = jax.nn.softmax(jnp.einsum('td,de->te', xt, w_router, preferred_element_type=jnp.float32), axis=-1)
    gate, idx = lax.top_k(aff.T, cap)
    xe = xt[idx]
    h = jax.nn.silu(jnp.einsum('ecd,edf->ecf', xe, w_gate)) * jnp.einsum('ecd,edf->ecf', xe, w_up)
    ye = jnp.einsum('ecf,efd->ecd', h, w_down) * gate[..., None].astype(x.dtype)
    out = jnp.zeros((t, d), x.dtype).at[idx.reshape(-1)].add(ye.reshape(-1, d))
    return out.reshape(b, s, d)


def _encoder_layer(x, w_in, w_pool, pool_scale, q_norm_g, w_uq, kv_norm_g, w_ukv, w_o,
                   ln1_g, ln1_b, w_router, w_gate, w_up, w_down, ln2_g, ln2_b):
    proj = jnp.einsum('bsd,df->bsf', x, w_in)
    o1 = POOL_DIM
    o2 = o1 + Q_LORA_RANK
    o3 = o2 + KV_LORA_RANK
    pool_out = _pool_mixer(proj[..., :o1], w_pool, pool_scale)
    attn_out = _mla(proj[..., o1:o2], proj[..., o2:o3], proj[..., o3:], q_norm_g, w_uq, kv_norm_g, w_ukv)
    mix = jnp.einsum('bsm,md->bsd', jnp.concatenate([pool_out, attn_out], axis=-1), w_o)
    x = _layer_norm(DEEPNORM_ALPHA * x + mix, ln1_g, ln1_b)
    x = _layer_norm(DEEPNORM_ALPHA * x + _expert_choice_ffn(x, w_router, w_gate, w_up, w_down), ln2_g, ln2_b)
    return x


def setup_inputs(seed: int = 0) -> dict:
    key = jax.random.key(seed)
    ks = jax.random.split(key, 24)
    f32 = jnp.float32
    L = DEPTH

    def nrm(k, shape, scale):
        return jax.random.normal(k, shape, f32) * scale

    return {
        "x_prompt": jax.random.normal(ks[0], (BATCH, SEQ, D_MODEL), f32),
        "x_sample": jax.random.normal(ks[1], (DEC_BATCH, DEC_SEQ, D_MODEL), f32),
        "ln_in_g": 1.0 + nrm(ks[2], (D_MODEL,), 0.02),
        "ln_in_b": nrm(ks[3], (D_MODEL,), 0.02),
        "w_in": nrm(ks[4], (L, D_MODEL, IN_PROJ_DIM), D_MODEL ** -0.5),
        "w_pool": nrm(ks[5], (L, N_POOL_GROUPS, POOL_GROUP_DIM, POOL_GROUP_DIM), POOL_GROUP_DIM ** -0.5),
        "pool_scale": 1.0 + nrm(ks[6], (L, POOL_DIM), 0.1),
        "q_norm_g": 1.0 + nrm(ks[7], (L, Q_LORA_RANK), 0.02),
        "w_uq": nrm(ks[8], (L, Q_LORA_RANK, N_HEADS * (QK_NOPE_DIM + QK_ROPE_DIM)), Q_LORA_RANK ** -0.5),
        "kv_norm_g": 1.0 + nrm(ks[9], (L, KV_LORA_RANK), 0.02),
        "w_ukv": nrm(ks[10], (L, KV_LORA_RANK, N_HEADS * (QK_NOPE_DIM + V_HEAD_DIM)), KV_LORA_RANK ** -0.5),
        "w_o": nrm(ks[11], (L, MIX_DIM, D_MODEL), MIX_DIM ** -0.5 * DEEPNORM_BETA),
        "ln1_g": 1.0 + nrm(ks[12], (L, D_MODEL), 0.02),
        "ln1_b": nrm(ks[13], (L, D_MODEL), 0.02),
        "w_router": nrm(ks[14], (L, D_MODEL, N_EXPERTS), D_MODEL ** -0.5),
        "w_gate": nrm(ks[15], (L, N_EXPERTS, D_MODEL, FFN_HIDDEN), D_MODEL ** -0.5),
        "w_up": nrm(ks[16], (L, N_EXPERTS, D_MODEL, FFN_HIDDEN), D_MODEL ** -0.5),
        "w_down": nrm(ks[17], (L, N_EXPERTS, FFN_HIDDEN, D_MODEL), FFN_HIDDEN ** -0.5 * DEEPNORM_BETA),
        "ln2_g": 1.0 + nrm(ks[18], (L, D_MODEL), 0.02),
        "ln2_b": nrm(ks[19], (L, D_MODEL), 0.02),
    }


def _trunk(x, ln_in_g, ln_in_b, w_in, w_pool, pool_scale, q_norm_g, w_uq, kv_norm_g, w_ukv, w_o,
           ln1_g, ln1_b, w_router, w_gate, w_up, w_down, ln2_g, ln2_b):
    x = _layer_norm(x, ln_in_g, ln_in_b)
    for l in range(DEPTH):
        x = _encoder_layer(x, w_in[l], w_pool[l], pool_scale[l], q_norm_g[l], w_uq[l], kv_norm_g[l], w_ukv[l],
                           w_o[l], ln1_g[l], ln1_b[l], w_router[l], w_gate[l], w_up[l], w_down[l],
                           ln2_g[l], ln2_b[l])
    return x


def reference(x_prompt, x_sample, ln_in_g, ln_in_b, w_in, w_pool, pool_scale, q_norm_g, w_uq, kv_norm_g,
              w_ukv, w_o, ln1_g, ln1_b, w_router, w_gate, w_up, w_down, ln2_g, ln2_b):
    y_prompt = _trunk(x_prompt, ln_in_g, ln_in_b, w_in, w_pool, pool_scale, q_norm_g, w_uq, kv_norm_g, w_ukv,
                      w_o, ln1_g, ln1_b, w_router, w_gate, w_up, w_down, ln2_g, ln2_b)
    y_sample = _trunk(x_sample, ln_in_g, ln_in_b, w_in, w_pool, pool_scale, q_norm_g, w_uq, kv_norm_g, w_ukv,
                      w_o, ln1_g, ln1_b, w_router, w_gate, w_up, w_down, ln2_g, ln2_b)
    return (y_prompt, y_sample)
```

```python
import functools
import math

import jax
import jax.numpy as jnp
from jax import lax
from jax.experimental import pallas as pl
from jax.experimental.pallas import tpu as pltpu

D_MODEL = 1024
POOL_WINDOWS = (2, 4, 8, 16)
POOL_DIM = 512
POOL_GROUP_DIM = 128
N_HEADS = 8
QK_NOPE_DIM = 64
QK_ROPE_DIM = 32
V_HEAD_DIM = 64
ATTN_DIM = N_HEADS * V_HEAD_DIM
Q_LORA_RANK = 384
KV_LORA_RANK = 256
ROPE_THETA = 10000.0
N_EXPERTS = 16
CAPACITY_FACTOR = 2
FFN_HIDDEN = 512
LN_EPS = 1e-5
RMS_EPS = 1e-6
DEPTH = 1
DEEPNORM_ALPHA = (2.0 * DEPTH) ** 0.25

LANES = 128
HEAD_PAD = 128
HALO = 8
VMEM_LIMIT = 56 * 1024 * 1024

BF16 = jnp.bfloat16
F32 = jnp.float32


def _layer_norm(x, g, b):
    mu = jnp.mean(x, axis=-1, keepdims=True)
    xc = x - mu
    var = jnp.mean(xc * xc, axis=-1, keepdims=True)
    return xc * lax.rsqrt(var + LN_EPS) * g + b


def _rms_norm(x, g):
    return x * lax.rsqrt(jnp.mean(x * x, axis=-1, keepdims=True) + RMS_EPS) * g


def _proj_kernel(x_ref, lng_ref, lnb_ref, win_ref, qg_ref, wuq_ref, kvg_ref, wk_ref, wv_ref,
                 cos_ref, sin_ref, u_ref, q_ref, k_ref, v_ref):
    xn = _layer_norm(x_ref[0], lng_ref[...], lnb_ref[...])
    proj = jnp.dot(xn.astype(BF16), win_ref[...], preferred_element_type=F32)
    o1 = POOL_DIM
    o2 = o1 + Q_LORA_RANK
    o3 = o2 + KV_LORA_RANK
    u_ref[0] = proj[:, :o1]
    qn = _rms_norm(proj[:, o1:o2], qg_ref[...])
    q = jnp.dot(qn.astype(BF16), wuq_ref[...], preferred_element_type=F32)
    kvn = _rms_norm(proj[:, o2:o3], kvg_ref[...]).astype(BF16)
    kk = jnp.dot(kvn, wk_ref[...], preferred_element_type=F32)
    vv = jnp.dot(kvn, wv_ref[...], preferred_element_type=F32)
    cos = cos_ref[...]
    sin = sin_ref[...]
    lane = lax.broadcasted_iota(jnp.int32, cos.shape, 1)
    first_half = lane < QK_NOPE_DIM + QK_ROPE_DIM // 2

    def rope(t):
        partner = jnp.where(first_half,
                            pltpu.roll(t, HEAD_PAD - QK_ROPE_DIM // 2, 1),
                            pltpu.roll(t, QK_ROPE_DIM // 2, 1))
        return t * cos + partner * sin

    k_rope = rope(proj[:, o3:o3 + HEAD_PAD])
    ones_tail = jnp.where(lane >= V_HEAD_DIM, 1.0, 0.0).astype(F32)
    scale = 1.0 / math.sqrt(QK_NOPE_DIM + QK_ROPE_DIM)
    for h in range(N_HEADS):
        sl = slice(h * HEAD_PAD, (h + 1) * HEAD_PAD)
        q_ref[0, h] = (rope(q[:, sl]) * scale).astype(BF16)
        k_ref[0, h] = (kk[:, sl] + k_rope).astype(BF16)
        v_ref[0, h] = (vv[:, sl] + ones_tail).astype(BF16)


def _proj_call(x, p, tm):
    b, s, _ = x.shape
    full = lambda shape: pl.BlockSpec(shape, lambda i, j: (0,) * len(shape))
    hs = pl.BlockSpec((1, N_HEADS, tm, HEAD_PAD), lambda i, j: (i, 0, j, 0))
    return pl.pallas_call(
        _proj_kernel,
        grid=(b, s // tm),
        in_specs=[
            pl.BlockSpec((1, tm, D_MODEL), lambda i, j: (i, j, 0)),
            full((1, D_MODEL)), full((1, D_MODEL)),
            full(p["w_in"].shape),
            full((1, Q_LORA_RANK)), full(p["w_uq"].shape),
            full((1, KV_LORA_RANK)), full(p["w_k"].shape), full(p["w_v"].shape),
            pl.BlockSpec((tm, HEAD_PAD), lambda i, j: (j, 0)),
            pl.BlockSpec((tm, HEAD_PAD), lambda i, j: (j, 0)),
        ],
        out_specs=[
            pl.BlockSpec((1, tm, POOL_DIM), lambda i, j: (i, j, 0)),
            hs, hs, hs,
        ],
        out_shape=[
            jax.ShapeDtypeStruct((b, s, POOL_DIM), F32),
            jax.ShapeDtypeStruct((b, N_HEADS, s, HEAD_PAD), BF16),
            jax.ShapeDtypeStruct((b, N_HEADS, s, HEAD_PAD), BF16),
            jax.ShapeDtypeStruct((b, N_HEADS, s, HEAD_PAD), BF16),
        ],
        compiler_params=pltpu.CompilerParams(
            dimension_semantics=("arbitrary", "arbitrary"), vmem_limit_bytes=VMEM_LIMIT),
        name="proj",
    )(x, p["ln_in_g"], p["ln_in_b"], p["w_in"], p["q_norm_g"], p["w_uq"], p["kv_norm_g"],
      p["w_k"], p["w_v"], p["cos"][s], p["sin"][s])


def _attn_kernel(q_ref, k_ref, v_ref, o_ref, *, tk):
    s_len = k_ref.shape[2]
    tq = q_ref.shape[2]
    heads = []
    for hh in range(2):
        q = q_ref[0, hh]

        def body(j, carry, hh=hh, q=q):
            m, acc = carry
            start = pl.multiple_of(j * tk, tk)
            k = k_ref[0, hh, pl.ds(start, tk), :]
            v = v_ref[0, hh, pl.ds(start, tk), :]
            s = lax.dot_general(q, k, (((1,), (1,)), ((), ())), preferred_element_type=F32)
            m_new = jnp.maximum(m, jnp.max(s, axis=-1, keepdims=True))
            p = jnp.exp(s - m_new)
            acc = acc * jnp.exp(m - m_new) + jnp.dot(p.astype(BF16), v, preferred_element_type=F32)
            return m_new, acc

        m0 = jnp.full((tq, 1), -jnp.inf, F32)
        acc0 = jnp.zeros((tq, HEAD_PAD), F32)
        _, acc = lax.fori_loop(0, s_len // tk, body, (m0, acc0))
        heads.append(acc / acc[:, V_HEAD_DIM:V_HEAD_DIM + 1])
    lane = lax.broadcasted_iota(jnp.int32, (tq, HEAD_PAD), 1)
    out = jnp.where(lane < V_HEAD_DIM, heads[0], pltpu.roll(heads[1], V_HEAD_DIM, 1))
    o_ref[0] = out.astype(BF16)


def _attn_call(q, k, v, tq, tk):
    b, h, s, _ = q.shape
    kv_spec = pl.BlockSpec((1, 2, s, HEAD_PAD), lambda i, j, t: (i, j, 0, 0))
    return pl.pallas_call(
        functools.partial(_attn_kernel, tk=tk),
        grid=(b, h // 2, s // tq),
        in_specs=[pl.BlockSpec((1, 2, tq, HEAD_PAD), lambda i, j, t: (i, j, t, 0)), kv_spec, kv_spec],
        out_specs=pl.BlockSpec((1, tq, 2 * V_HEAD_DIM), lambda i, j, t: (i, t, j)),
        out_shape=jax.ShapeDtypeStruct((b, s, ATTN_DIM), BF16),
        compiler_params=pltpu.CompilerParams(
            dimension_semantics=("arbitrary", "arbitrary", "arbitrary"), vmem_limit_bytes=VMEM_LIMIT),
        name="attn",
    )(q, k, v)


def _mix_kernel(x_ref, up_ref, um_ref, un_ref, at_ref, lng_ref, lnb_ref, wp_ref, ps_ref, wo_ref,
                g1_ref, b1_ref, wr_ref, x1_ref, x1b_ref, aff_ref, ubuf):
    tm = um_ref.shape[1]
    j = pl.program_id(1)
    n_tiles = pl.num_programs(1)
    s_len = n_tiles * tm
    xn = _layer_norm(x_ref[0], lng_ref[...], lnb_ref[...])

    ubuf[pl.ds(0, HALO), :] = jnp.where(j > 0, up_ref[0], 0.0)
    ubuf[pl.ds(HALO, tm), :] = um_ref[0]
    ubuf[pl.ds(HALO + tm, HALO), :] = jnp.where(j < n_tiles - 1, un_ref[0], 0.0)

    pos = j * tm + lax.broadcasted_iota(jnp.int32, (tm, 1), 0)
    mix = jnp.dot(at_ref[0], wo_ref[pl.ds(POOL_DIM, ATTN_DIM), :], preferred_element_type=F32)
    for g, w in enumerate(POOL_WINDOWS):
        cols = pl.ds(g * POOL_GROUP_DIM, POOL_GROUP_DIM)
        win_sum = ubuf[pl.ds(HALO - w // 2, tm), cols]
        for off in range(-w // 2 + 1, w // 2):
            win_sum = win_sum + ubuf[pl.ds(HALO + off, tm), cols]
        lo = jnp.maximum(pos - w // 2, 0)
        hi = jnp.minimum(pos + w // 2 - 1, s_len - 1)
        cnt = (hi - lo + 1).astype(F32)
        d = win_sum / cnt - ubuf[pl.ds(HALO, tm), cols]
        y = jnp.dot(d.astype(BF16), wp_ref[g], preferred_element_type=F32) * ps_ref[:, cols]
        mix = mix + jnp.dot(y.astype(BF16), wo_ref[cols, :], preferred_element_type=F32)

    x1 = _layer_norm(DEEPNORM_ALPHA * xn + mix, g1_ref[...], b1_ref[...])
    x1_ref[0] = x1
    x1b_ref[0] = x1.astype(BF16)
    logits = jnp.dot(x1, wr_ref[...], preferred_element_type=F32, precision=lax.Precision.HIGHEST)
    e = jnp.exp(logits - jnp.max(logits, axis=-1, keepdims=True))
    aff_ref[0] = e / jnp.sum(e, axis=-1, keepdims=True)


def _mix_call(x, u, attn, p, tm):
    b, s, _ = x.shape
    full = lambda shape: pl.BlockSpec(shape, lambda i, j: (0,) * len(shape))
    tile = lambda width: pl.BlockSpec((1, tm, width), lambda i, j: (i, j, 0))
    hb = tm // HALO
    last_halo_block = s // HALO - 1
    return pl.pallas_call(
        _mix_kernel,
        grid=(b, s // tm),
        in_specs=[
            tile(D_MODEL),
            pl.BlockSpec((1, HALO, POOL_DIM), lambda i, j: (i, jnp.maximum(j * hb - 1, 0), 0)),
            tile(POOL_DIM),
            pl.BlockSpec((1, HALO, POOL_DIM), lambda i, j: (i, jnp.minimum((j + 1) * hb, last_halo_block), 0)),
            tile(ATTN_DIM),
            full((1, D_MODEL)), full((1, D_MODEL)),
            full(p["w_pool"].shape), full((1, POOL_DIM)), full(p["w_o"].shape),
            full((1, D_MODEL)), full((1, D_MODEL)), full(p["w_router"].shape),
        ],
        out_specs=[tile(D_MODEL), tile(D_MODEL), tile(N_EXPERTS)],
        out_shape=[
            jax.ShapeDtypeStruct((b, s, D_MODEL), F32),
            jax.ShapeDtypeStruct((b, s, D_MODEL), BF16),
            jax.ShapeDtypeStruct((b, s, N_EXPERTS), F32),
        ],
        scratch_shapes=[pltpu.VMEM((tm + 2 * HALO, POOL_DIM), F32)],
        compiler_params=pltpu.CompilerParams(
            dimension_semantics=("arbitrary", "arbitrary"), vmem_limit_bytes=VMEM_LIMIT),
        name="mix",
    )(x, u, u, u, attn, p["ln_in_g"], p["ln_in_b"], p["w_pool"], p["pool_scale"], p["w_o"],
      p["ln1_g"], p["ln1_b"], p["w_router"])


def _select_kernel(aff_ref, gate_ref, *, cap, chunk):
    n_exp, t = aff_ref.shape
    n_chunks = t // chunk

    def count(pred):
        def body(c, acc):
            start = pl.multiple_of(c * chunk, chunk)
            bits = lax.bitcast_convert_type(aff_ref[:, pl.ds(start, chunk)], jnp.int32)
            tok = start + lax.broadcasted_iota(jnp.int32, (n_exp, chunk), 1)
            return acc + jnp.sum(jnp.where(pred(bits, tok), 1.0, 0.0), axis=1, keepdims=True)
        return lax.fori_loop(0, n_chunks, body, jnp.zeros((n_exp, 1), F32))

    def value_step(i, thr):
        cand = thr | (1 << (30 - i))
        n_ge = count(lambda bits, tok: bits >= cand)
        return jnp.where(n_ge >= cap, cand, thr)
    thr = lax.fori_loop(0, 31, value_step, jnp.zeros((n_exp, 1), jnp.int32))

    need = cap - count(lambda bits, tok: bits > thr)
    index_bits = t.bit_length()

    def index_step(i, below):
        cand = below + (1 << (index_bits - 1 - i))
        n_ties = count(lambda bits, tok: (bits == thr) & (tok < cand))
        return jnp.where((cand <= t) & (n_ties < need), cand, below)
    cut = lax.fori_loop(0, index_bits, index_step, jnp.zeros((n_exp, 1), jnp.int32)) + 1

    def write(c, carry):
        start = pl.multiple_of(c * chunk, chunk)
        aff = aff_ref[:, pl.ds(start, chunk)]
        bits = lax.bitcast_convert_type(aff, jnp.int32)
        tok = start + lax.broadcasted_iota(jnp.int32, (n_exp, chunk), 1)
        sel = (bits > thr) | ((bits == thr) & (tok < cut))
        gate_ref[:, pl.ds(start, chunk)] = jnp.where(sel, aff, 0.0)
        return carry
    lax.fori_loop(0, n_chunks, write, 0)


def _select_call(aff_t, cap):
    n_exp, t = aff_t.shape
    return pl.pallas_call(
        functools.partial(_select_kernel, cap=cap, chunk=2048),
        out_shape=jax.ShapeDtypeStruct((n_exp, t), F32),
        compiler_params=pltpu.CompilerParams(vmem_limit_bytes=VMEM_LIMIT),
        name="select",
    )(aff_t)


def _moe_kernel(xb_ref, x1_ref, gate_ref, wg_ref, wu_ref, wd_ref, g2_ref, b2_ref, o_ref, acc_ref):
    e = pl.program_id(1)

    @pl.when(e == 0)
    def _():
        acc_ref[...] = jnp.zeros_like(acc_ref)

    gates = gate_ref[...]
    lane = lax.broadcasted_iota(jnp.int32, gates.shape, 1)
    gate = jnp.sum(jnp.where(lane == e, gates, 0.0), axis=1, keepdims=True)

    @pl.when(jnp.max(gate) > 0.0)
    def _():
        xb = xb_ref[...]
        hg = jnp.dot(xb, wg_ref[0], preferred_element_type=F32)
        hu = jnp.dot(xb, wu_ref[0], preferred_element_type=F32)
        h = hg * jax.nn.sigmoid(hg) * hu
        ye = jnp.dot(h.astype(BF16), wd_ref[0], preferred_element_type=F32)
        acc_ref[...] += ye * gate

    @pl.when(e == pl.num_programs(1) - 1)
    def _():
        o_ref[...] = _layer_norm(DEEPNORM_ALPHA * x1_ref[...] + acc_ref[...], g2_ref[...], b2_ref[...])


def _moe_call(x1b, x1, gates, p, tm):
    t, _ = x1.shape
    full = lambda shape: pl.BlockSpec(shape, lambda i, e: (0,) * len(shape))
    return pl.pallas_call(
        _moe_kernel,
        grid=(t // tm, N_EXPERTS),
        in_specs=[
            pl.BlockSpec((tm, D_MODEL), lambda i, e: (i, 0)),
            pl.BlockSpec((tm, D_MODEL), lambda i, e: (i, 0)),
            pl.BlockSpec((tm, N_EXPERTS), lambda i, e: (i, 0)),
            pl.BlockSpec((1, D_MODEL, FFN_HIDDEN), lambda i, e: (e, 0, 0)),
            pl.BlockSpec((1, D_MODEL, FFN_HIDDEN), lambda i, e: (e, 0, 0)),
            pl.BlockSpec((1, FFN_HIDDEN, D_MODEL), lambda i, e: (e, 0, 0)),
            full((1, D_MODEL)), full((1, D_MODEL)),
        ],
        out_specs=pl.BlockSpec((tm, D_MODEL), lambda i, e: (i, 0)),
        out_shape=jax.ShapeDtypeStruct((t, D_MODEL), F32),
        scratch_shapes=[pltpu.VMEM((tm, D_MODEL), F32)],
        compiler_params=pltpu.CompilerParams(
            dimension_semantics=("arbitrary", "arbitrary"), vmem_limit_bytes=VMEM_LIMIT),
        name="moe",
    )(x1b, x1, gates, p["w_gate"], p["w_up"], p["w_down"], p["ln2_g"], p["ln2_b"])


def _rope_tables(s):
    half = QK_ROPE_DIM // 2
    inv = 1.0 / (ROPE_THETA ** (jnp.arange(0, QK_ROPE_DIM, 2, dtype=F32) / QK_ROPE_DIM))
    ang = jnp.arange(s, dtype=F32)[:, None] * inv[None, :]
    cos, sin = jnp.cos(ang), jnp.sin(ang)
    tail = HEAD_PAD - QK_NOPE_DIM - QK_ROPE_DIM
    cos_t = jnp.concatenate([jnp.ones((s, QK_NOPE_DIM), F32), cos, cos, jnp.ones((s, tail), F32)], axis=1)
    sin_t = jnp.concatenate([jnp.zeros((s, QK_NOPE_DIM), F32), -sin, sin, jnp.zeros((s, tail), F32)], axis=1)
    assert half * 2 == QK_ROPE_DIM
    return cos_t, sin_t


def _prepare_params(seq_lens, ln_in_g, ln_in_b, w_in, w_pool, pool_scale, q_norm_g, w_uq, kv_norm_g, w_ukv,
                    w_o, ln1_g, ln1_b, w_router, w_gate, w_up, w_down, ln2_g, ln2_b):
    row = lambda a: a.reshape(1, -1).astype(F32)
    o3 = POOL_DIM + Q_LORA_RANK + KV_LORA_RANK
    w_in0 = w_in[0]
    w_in_p = jnp.concatenate([
        w_in0[:, :o3], jnp.zeros((D_MODEL, QK_NOPE_DIM), F32), w_in0[:, o3:],
        jnp.zeros((D_MODEL, HEAD_PAD - QK_NOPE_DIM - QK_ROPE_DIM), F32)], axis=1)
    qk = QK_NOPE_DIM + QK_ROPE_DIM
    w_uq_p = jnp.pad(w_uq[0].reshape(Q_LORA_RANK, N_HEADS, qk), ((0, 0), (0, 0), (0, HEAD_PAD - qk)))
    w_ukv0 = w_ukv[0].reshape(KV_LORA_RANK, N_HEADS, QK_NOPE_DIM + V_HEAD_DIM)
    w_k = jnp.pad(w_ukv0[:, :, :QK_NOPE_DIM], ((0, 0), (0, 0), (0, HEAD_PAD - QK_NOPE_DIM)))
    w_v = jnp.pad(w_ukv0[:, :, QK_NOPE_DIM:], ((0, 0), (0, 0), (0, HEAD_PAD - V_HEAD_DIM)))
    tables = {s: _rope_tables(s) for s in seq_lens}
    return {
        "ln_in_g": row(ln_in_g), "ln_in_b": row(ln_in_b),
        "w_in": w_in_p.astype(BF16),
        "w_pool": w_pool[0].astype(BF16), "pool_scale": row(pool_scale[0]),
        "q_norm_g": row(q_norm_g[0]), "w_uq": w_uq_p.reshape(Q_LORA_RANK, N_HEADS * HEAD_PAD).astype(BF16),
        "kv_norm_g": row(kv_norm_g[0]),
        "w_k": w_k.reshape(KV_LORA_RANK, N_HEADS * HEAD_PAD).astype(BF16),
        "w_v": w_v.reshape(KV_LORA_RANK, N_HEADS * HEAD_PAD).astype(BF16),
        "w_o": w_o[0].astype(BF16),
        "ln1_g": row(ln1_g[0]), "ln1_b": row(ln1_b[0]),
        "w_router": w_router[0].astype(F32),
        "w_gate": w_gate[0].astype(BF16), "w_up": w_up[0].astype(BF16), "w_down": w_down[0].astype(BF16),
        "ln2_g": row(ln2_g[0]), "ln2_b": row(ln2_b[0]),
        "cos": {s: t[0] for s, t in tables.items()},
        "sin": {s: t[1] for s, t in tables.items()},
    }


def _trunk(x, p):
    b, s, d = x.shape
    t = b * s
    u, q, k, v = _proj_call(x, p, tm=512)
    attn = _attn_call(q, k, v, tq=512, tk=512)
    x1, x1b, aff = _mix_call(x, u, attn, p, tm=512)
    cap = max(1, (CAPACITY_FACTOR * t) // N_EXPERTS)
    gates_t = _select_call(aff.reshape(t, N_EXPERTS).T, cap)
    y = _moe_call(x1b.reshape(t, d), x1.reshape(t, d), gates_t.T, p, tm=1024)
    return y.reshape(b, s, d)


def kernel(x_prompt, x_sample, ln_in_g, ln_in_b, w_in, w_pool, pool_scale, q_norm_g, w_uq, kv_norm_g, w_ukv,
           w_o, ln1_g, ln1_b, w_router, w_gate, w_up, w_down, ln2_g, ln2_b):
    assert w_in.shape[0] == DEPTH == 1
    p = _prepare_params((x_prompt.shape[1], x_sample.shape[1]), ln_in_g, ln_in_b, w_in, w_pool, pool_scale,
                        q_norm_g, w_uq, kv_norm_g, w_ukv, w_o, ln1_g, ln1_b, w_router, w_gate, w_up, w_down,
                        ln2_g, ln2_b)
    return (_trunk(x_prompt, p), _trunk(x_sample, p))
```

```python
import functools
import math

import jax
import jax.numpy as jnp
from jax import lax
from jax.experimental import pallas as pl
from jax.experimental.pallas import tpu as pltpu

D_MODEL = 1024
POOL_WINDOWS = (2, 4, 8, 16)
POOL_DIM = 512
POOL_GROUP_DIM = 128
N_HEADS = 8
QK_NOPE_DIM = 64
QK_ROPE_DIM = 32
V_HEAD_DIM = 64
ATTN_DIM = N_HEADS * V_HEAD_DIM
Q_LORA_RANK = 384
KV_LORA_RANK = 256
ROPE_THETA = 10000.0
N_EXPERTS = 16
CAPACITY_FACTOR = 2
FFN_HIDDEN = 512
LN_EPS = 1e-5
RMS_EPS = 1e-6
DEPTH = 1
DEEPNORM_ALPHA = (2.0 * DEPTH) ** 0.25

LANES = 128
HEAD_PAD = 128
HEADS_PER_STEP = 2
HALO = 8
VMEM_LIMIT = 56 * 1024 * 1024

BF16 = jnp.bfloat16
F32 = jnp.float32


def _layer_norm(x, g, b):
    mu = jnp.mean(x, axis=-1, keepdims=True)
    xc = x - mu
    var = jnp.mean(xc * xc, axis=-1, keepdims=True)
    return xc * lax.rsqrt(var + LN_EPS) * g + b


def _rms_norm(x, g):
    return x * lax.rsqrt(jnp.mean(x * x, axis=-1, keepdims=True) + RMS_EPS) * g


def _proj_kernel(x_ref, lng_ref, lnb_ref, win_ref, qg_ref, wuq_ref, kvg_ref, wk_ref, wvt_ref,
                 cos_ref, sin_ref, u_ref, q_ref, k_ref, vt_ref):
    xn = _layer_norm(x_ref[0], lng_ref[...], lnb_ref[...])
    proj = jnp.dot(xn.astype(BF16), win_ref[...], preferred_element_type=F32)
    o1 = POOL_DIM
    o2 = o1 + Q_LORA_RANK
    o3 = o2 + KV_LORA_RANK
    u_ref[0] = proj[:, :o1]
    qn = _rms_norm(proj[:, o1:o2], qg_ref[...])
    q = jnp.dot(qn.astype(BF16), wuq_ref[...], preferred_element_type=F32)
    kvn = _rms_norm(proj[:, o2:o3], kvg_ref[...]).astype(BF16)
    kk = jnp.dot(kvn, wk_ref[...], preferred_element_type=F32)
    vvt = lax.dot_general(wvt_ref[...], kvn, (((1,), (1,)), ((), ())),
                          preferred_element_type=F32)
    cos = cos_ref[...]
    sin = sin_ref[...]
    lane = lax.broadcasted_iota(jnp.int32, cos.shape, 1)
    first_half = lane < QK_NOPE_DIM + QK_ROPE_DIM // 2

    def rope(t):
        partner = jnp.where(first_half,
                            pltpu.roll(t, HEAD_PAD - QK_ROPE_DIM // 2, 1),
                            pltpu.roll(t, QK_ROPE_DIM // 2, 1))
        return t * cos + partner * sin

    k_rope = rope(proj[:, o3:o3 + HEAD_PAD])
    row = lax.broadcasted_iota(jnp.int32, (HEAD_PAD, vvt.shape[1]), 0)
    ones_rows = jnp.where(row >= V_HEAD_DIM, 1.0, 0.0).astype(F32)
    scale = math.log2(math.e) / math.sqrt(QK_NOPE_DIM + QK_ROPE_DIM)
    for h in range(N_HEADS):
        sl = slice(h * HEAD_PAD, (h + 1) * HEAD_PAD)
        q_ref[0, h] = (rope(q[:, sl]) * scale).astype(BF16)
        k_ref[0, h] = (kk[:, sl] + k_rope).astype(BF16)
        vt_ref[0, h] = (vvt[sl, :] + ones_rows).astype(BF16)


def _proj_call(x, p, tm):
    b, s, _ = x.shape
    full = lambda shape: pl.BlockSpec(shape, lambda i, j: (0,) * len(shape))
    hs = pl.BlockSpec((1, N_HEADS, tm, HEAD_PAD), lambda i, j: (i, 0, j, 0))
    return pl.pallas_call(
        _proj_kernel,
        grid=(b, s // tm),
        in_specs=[
            pl.BlockSpec((1, tm, D_MODEL), lambda i, j: (i, j, 0)),
            full((1, D_MODEL)), full((1, D_MODEL)),
            full(p["w_in"].shape),
            full((1, Q_LORA_RANK)), full(p["w_uq"].shape),
            full((1, KV_LORA_RANK)), full(p["w_k"].shape), full(p["w_vt"].shape),
            pl.BlockSpec((tm, HEAD_PAD), lambda i, j: (j, 0)),
            pl.BlockSpec((tm, HEAD_PAD), lambda i, j: (j, 0)),
        ],
        out_specs=[
            pl.BlockSpec((1, tm, POOL_DIM), lambda i, j: (i, j, 0)),
            hs, hs, pl.BlockSpec((1, N_HEADS, HEAD_PAD, tm), lambda i, j: (i, 0, 0, j)),
        ],
        out_shape=[
            jax.ShapeDtypeStruct((b, s, POOL_DIM), F32),
            jax.ShapeDtypeStruct((b, N_HEADS, s, HEAD_PAD), BF16),
            jax.ShapeDtypeStruct((b, N_HEADS, s, HEAD_PAD), BF16),
            jax.ShapeDtypeStruct((b, N_HEADS, HEAD_PAD, s), BF16),
        ],
        compiler_params=pltpu.CompilerParams(
            dimension_semantics=("arbitrary", "arbitrary"), vmem_limit_bytes=VMEM_LIMIT),
        name="proj",
    )(x, p["ln_in_g"], p["ln_in_b"], p["w_in"], p["q_norm_g"], p["w_uq"], p["kv_norm_g"],
      p["w_k"], p["w_vt"], p["cos"][s], p["sin"][s])


def _attn_kernel(q_ref, k_ref, vt_ref, o_ref, m_ref, acc_ref, s_ref, p_ref, alpha_ref, *, tk):
    s_len = k_ref.shape[2]
    tq = q_ref.shape[2]
    n_chunks = s_len // tk
    m_ref[...] = jnp.full(m_ref.shape, -jnp.inf, F32)
    acc_ref[...] = jnp.zeros(acc_ref.shape, F32)

    def scores(j, slot):
        start = j * tk
        for hh in range(HEADS_PER_STEP):
            k = k_ref[0, hh, pl.ds(start, tk), :]
            s_ref[slot, hh] = lax.dot_general(k, q_ref[0, hh], (((1,), (1,)), ((), ())),
                                              preferred_element_type=F32)

    def softmax(slot):
        for hh in range(HEADS_PER_STEP):
            st = s_ref[slot, hh]
            m_old = m_ref[hh]
            m_new = jnp.maximum(m_old, jnp.max(st, axis=0, keepdims=True))
            p_ref[slot, hh] = jnp.exp2(st - m_new).astype(BF16)
            alpha_ref[slot, hh] = jnp.exp2(m_old - m_new)
            m_ref[hh] = m_new

    def values(j, slot):
        start = j * tk
        for hh in range(HEADS_PER_STEP):
            vt = vt_ref[0, hh, :, pl.ds(start, tk)]
            acc_ref[hh] = acc_ref[hh] * alpha_ref[slot, hh] + jnp.dot(
                vt, p_ref[slot, hh], preferred_element_type=F32)

    scores(0, 0)
    for j in range(n_chunks):
        softmax(j % 2)
        if j + 1 < n_chunks:
            scores(j + 1, (j + 1) % 2)
        if j >= 1:
            values(j - 1, (j - 1) % 2)
    values(n_chunks - 1, (n_chunks - 1) % 2)
    heads = []
    for hh in range(HEADS_PER_STEP):
        acc = acc_ref[hh]
        heads.append((acc / acc[V_HEAD_DIM:V_HEAD_DIM + 1, :]).T)
    lane = lax.broadcasted_iota(jnp.int32, (tq, HEAD_PAD), 1)
    out = jnp.where(lane < V_HEAD_DIM, heads[0], pltpu.roll(heads[1], V_HEAD_DIM, 1))
    o_ref[0] = out.astype(BF16)


def _attn_call(q, k, vt, tq, tk):
    b, h, s, _ = q.shape
    hp = HEADS_PER_STEP
    return pl.pallas_call(
        functools.partial(_attn_kernel, tk=tk),
        grid=(b, h // hp, s // tq),
        in_specs=[
            pl.BlockSpec((1, hp, tq, HEAD_PAD), lambda i, j, t: (i, j, t, 0)),
            pl.BlockSpec((1, hp, s, HEAD_PAD), lambda i, j, t: (i, j, 0, 0)),
            pl.BlockSpec((1, hp, HEAD_PAD, s), lambda i, j, t: (i, j, 0, 0)),
        ],
        out_specs=pl.BlockSpec((1, tq, hp * V_HEAD_DIM), lambda i, j, t: (i, t, j)),
        out_shape=jax.ShapeDtypeStruct((b, s, ATTN_DIM), BF16),
        scratch_shapes=[
            pltpu.VMEM((hp, 1, tq), F32),
            pltpu.VMEM((hp, HEAD_PAD, tq), F32),
            pltpu.VMEM((2, hp, tk, tq), F32),
            pltpu.VMEM((2, hp, tk, tq), BF16),
            pltpu.VMEM((2, hp, 1, tq), F32),
        ],
        compiler_params=pltpu.CompilerParams(
            dimension_semantics=("arbitrary", "arbitrary", "arbitrary"), vmem_limit_bytes=VMEM_LIMIT),
        name="attn",
    )(q, k, vt)


def _mix_kernel(x_ref, up_ref, um_ref, un_ref, at_ref, lng_ref, lnb_ref, wp_ref, ps_ref, wo_ref,
                g1_ref, b1_ref, wr_ref, x1_ref, x1b_ref, aff_ref, ubuf):
    tm = um_ref.shape[1]
    j = pl.program_id(1)
    n_tiles = pl.num_programs(1)
    s_len = n_tiles * tm
    xn = _layer_norm(x_ref[0], lng_ref[...], lnb_ref[...])

    ubuf[pl.ds(0, HALO), :] = jnp.where(j > 0, up_ref[0], 0.0)
    ubuf[pl.ds(HALO, tm), :] = um_ref[0]
    ubuf[pl.ds(HALO + tm, HALO), :] = jnp.where(j < n_tiles - 1, un_ref[0], 0.0)

    pos = j * tm + lax.broadcasted_iota(jnp.int32, (tm, 1), 0)
    mix = jnp.dot(at_ref[0], wo_ref[pl.ds(POOL_DIM, ATTN_DIM), :], preferred_element_type=F32)
    for g, w in enumerate(POOL_WINDOWS):
        cols = pl.ds(g * POOL_GROUP_DIM, POOL_GROUP_DIM)
        win_sum = ubuf[pl.ds(HALO - w // 2, tm), cols]
        for off in range(-w // 2 + 1, w // 2):
            win_sum = win_sum + ubuf[pl.ds(HALO + off, tm), cols]
        lo = jnp.maximum(pos - w // 2, 0)
        hi = jnp.minimum(pos + w // 2 - 1, s_len - 1)
        cnt = (hi - lo + 1).astype(F32)
        d = win_sum / cnt - ubuf[pl.ds(HALO, tm), cols]
        y = jnp.dot(d.astype(BF16), wp_ref[g], preferred_element_type=F32) * ps_ref[:, cols]
        mix = mix + jnp.dot(y.astype(BF16), wo_ref[cols, :], preferred_element_type=F32)

    x1 = _layer_norm(DEEPNORM_ALPHA * xn + mix, g1_ref[...], b1_ref[...])
    x1_ref[0] = x1
    x1_hi = x1.astype(BF16)
    x1b_ref[0] = x1_hi
    x1_lo = (x1 - x1_hi.astype(F32)).astype(BF16)
    logits = (jnp.dot(x1_hi, wr_ref[0], preferred_element_type=F32)
              + jnp.dot(x1_hi, wr_ref[1], preferred_element_type=F32)
              + jnp.dot(x1_lo, wr_ref[0], preferred_element_type=F32))
    e = jnp.exp(logits - jnp.max(logits, axis=-1, keepdims=True))
    aff_ref[0] = e / jnp.sum(e, axis=-1, keepdims=True)


def _mix_call(x, u, attn, p, tm):
    b, s, _ = x.shape
    full = lambda shape: pl.BlockSpec(shape, lambda i, j: (0,) * len(shape))
    tile = lambda width: pl.BlockSpec((1, tm, width), lambda i, j: (i, j, 0))
    hb = tm // HALO
    last_halo_block = s // HALO - 1
    return pl.pallas_call(
        _mix_kernel,
        grid=(b, s // tm),
        in_specs=[
            tile(D_MODEL),
            pl.BlockSpec((1, HALO, POOL_DIM), lambda i, j: (i, jnp.maximum(j * hb - 1, 0), 0)),
            tile(POOL_DIM),
            pl.BlockSpec((1, HALO, POOL_DIM), lambda i, j: (i, jnp.minimum((j + 1) * hb, last_halo_block), 0)),
            tile(ATTN_DIM),
            full((1, D_MODEL)), full((1, D_MODEL)),
            full(p["w_pool"].shape), full((1, POOL_DIM)), full(p["w_o"].shape),
            full((1, D_MODEL)), full((1, D_MODEL)), full(p["w_router"].shape),
        ],
        out_specs=[tile(D_MODEL), tile(D_MODEL), tile(N_EXPERTS)],
        out_shape=[
            jax.ShapeDtypeStruct((b, s, D_MODEL), F32),
            jax.ShapeDtypeStruct((b, s, D_MODEL), BF16),
            jax.ShapeDtypeStruct((b, s, N_EXPERTS), F32),
        ],
        scratch_shapes=[pltpu.VMEM((tm + 2 * HALO, POOL_DIM), F32)],
        compiler_params=pltpu.CompilerParams(
            dimension_semantics=("arbitrary", "arbitrary"), vmem_limit_bytes=VMEM_LIMIT),
        name="mix",
    )(x, u, u, u, attn, p["ln_in_g"], p["ln_in_b"], p["w_pool"], p["pool_scale"], p["w_o"],
      p["ln1_g"], p["ln1_b"], p["w_router"])


def _select_kernel(aff_ref, gate_ref, *, cap, chunk):
    n_exp, t = aff_ref.shape
    n_chunks = t // chunk

    def count(pred):
        def body(c, acc):
            start = pl.multiple_of(c * chunk, chunk)
            bits = lax.bitcast_convert_type(aff_ref[:, pl.ds(start, chunk)], jnp.int32)
            tok = start + lax.broadcasted_iota(jnp.int32, (n_exp, chunk), 1)
            return acc + jnp.sum(jnp.where(pred(bits, tok), 1.0, 0.0), axis=1, keepdims=True)
        return lax.fori_loop(0, n_chunks, body, jnp.zeros((n_exp, 1), F32))

    def value_step(i, thr):
        cand = thr | (1 << (30 - i))
        n_ge = count(lambda bits, tok: bits >= cand)
        return jnp.where(n_ge >= cap, cand, thr)
    thr = lax.fori_loop(0, 31, value_step, jnp.zeros((n_exp, 1), jnp.int32))

    need = cap - count(lambda bits, tok: bits > thr)
    index_bits = t.bit_length()

    def index_step(i, below):
        cand = below + (1 << (index_bits - 1 - i))
        n_ties = count(lambda bits, tok: (bits == thr) & (tok < cand))
        return jnp.where((cand <= t) & (n_ties < need), cand, below)
    cut = lax.fori_loop(0, index_bits, index_step, jnp.zeros((n_exp, 1), jnp.int32)) + 1

    def write(c, carry):
        start = pl.multiple_of(c * chunk, chunk)
        aff = aff_ref[:, pl.ds(start, chunk)]
        bits = lax.bitcast_convert_type(aff, jnp.int32)
        tok = start + lax.broadcasted_iota(jnp.int32, (n_exp, chunk), 1)
        sel = (bits > thr) | ((bits == thr) & (tok < cut))
        gate_ref[:, pl.ds(start, chunk)] = jnp.where(sel, aff, 0.0)
        return carry
    lax.fori_loop(0, n_chunks, write, 0)


def _select_call(aff_t, cap):
    n_exp, t = aff_t.shape
    return pl.pallas_call(
        functools.partial(_select_kernel, cap=cap, chunk=2048),
        out_shape=jax.ShapeDtypeStruct((n_exp, t), F32),
        compiler_params=pltpu.CompilerParams(vmem_limit_bytes=VMEM_LIMIT),
        name="select",
    )(aff_t)


def _moe_kernel(xb_ref, x1_ref, gate_ref, wg_ref, wu_ref, wd_ref, g2_ref, b2_ref, o_ref, acc_ref):
    e = pl.program_id(1)

    @pl.when(e == 0)
    def _():
        acc_ref[...] = jnp.zeros_like(acc_ref)

    gates = gate_ref[...]
    lane = lax.broadcasted_iota(jnp.int32, gates.shape, 1)
    gate = jnp.sum(jnp.where(lane == e, gates, 0.0), axis=1, keepdims=True)

    @pl.when(jnp.max(gate) > 0.0)
    def _():
        xb = xb_ref[...]
        hg = jnp.dot(xb, wg_ref[0], preferred_element_type=F32)
        hu = jnp.dot(xb, wu_ref[0], preferred_element_type=F32)
        h = hg * jax.nn.sigmoid(hg) * hu
        ye = jnp.dot(h.astype(BF16), wd_ref[0], preferred_element_type=F32)
        acc_ref[...] += ye * gate

    @pl.when(e == pl.num_programs(1) - 1)
    def _():
        o_ref[...] = _layer_norm(DEEPNORM_ALPHA * x1_ref[...] + acc_ref[...], g2_ref[...], b2_ref[...])


def _moe_call(x1b, x1, gates, p, tm):
    t, _ = x1.shape
    full = lambda shape: pl.BlockSpec(shape, lambda i, e: (0,) * len(shape))
    return pl.pallas_call(
        _moe_kernel,
        grid=(t // tm, N_EXPERTS),
        in_specs=[
            pl.BlockSpec((tm, D_MODEL), lambda i, e: (i, 0)),
            pl.BlockSpec((tm, D_MODEL), lambda i, e: (i, 0)),
            pl.BlockSpec((tm, N_EXPERTS), lambda i, e: (i, 0)),
            pl.BlockSpec((1, D_MODEL, FFN_HIDDEN), lambda i, e: (e, 0, 0)),
            pl.BlockSpec((1, D_MODEL, FFN_HIDDEN), lambda i, e: (e, 0, 0)),
            pl.BlockSpec((1, FFN_HIDDEN, D_MODEL), lambda i, e: (e, 0, 0)),
            full((1, D_MODEL)), full((1, D_MODEL)),
        ],
        out_specs=pl.BlockSpec((tm, D_MODEL), lambda i, e: (i, 0)),
        out_shape=jax.ShapeDtypeStruct((t, D_MODEL), F32),
        scratch_shapes=[pltpu.VMEM((tm, D_MODEL), F32)],
        compiler_params=pltpu.CompilerParams(
            dimension_semantics=("arbitrary", "arbitrary"), vmem_limit_bytes=VMEM_LIMIT),
        name="moe",
    )(x1b, x1, gates, p["w_gate"], p["w_up"], p["w_down"], p["ln2_g"], p["ln2_b"])


def _rope_tables(s):
    half = QK_ROPE_DIM // 2
    inv = 1.0 / (ROPE_THETA ** (jnp.arange(0, QK_ROPE_DIM, 2, dtype=F32) / QK_ROPE_DIM))
    ang = jnp.arange(s, dtype=F32)[:, None] * inv[None, :]
    cos, sin = jnp.cos(ang), jnp.sin(ang)
    tail = HEAD_PAD - QK_NOPE_DIM - QK_ROPE_DIM
    cos_t = jnp.concatenate([jnp.ones((s, QK_NOPE_DIM), F32), cos, cos, jnp.ones((s, tail), F32)], axis=1)
    sin_t = jnp.concatenate([jnp.zeros((s, QK_NOPE_DIM), F32), -sin, sin, jnp.zeros((s, tail), F32)], axis=1)
    assert half * 2 == QK_ROPE_DIM
    return cos_t, sin_t


def _prepare_params(seq_lens, ln_in_g, ln_in_b, w_in, w_pool, pool_scale, q_norm_g, w_uq, kv_norm_g, w_ukv,
                    w_o, ln1_g, ln1_b, w_router, w_gate, w_up, w_down, ln2_g, ln2_b):
    row = lambda a: a.reshape(1, -1).astype(F32)
    o3 = POOL_DIM + Q_LORA_RANK + KV_LORA_RANK
    w_in0 = w_in[0]
    w_in_p = jnp.concatenate([
        w_in0[:, :o3], jnp.zeros((D_MODEL, QK_NOPE_DIM), F32), w_in0[:, o3:],
        jnp.zeros((D_MODEL, HEAD_PAD - QK_NOPE_DIM - QK_ROPE_DIM), F32)], axis=1)
    qk = QK_NOPE_DIM + QK_ROPE_DIM
    w_uq_p = jnp.pad(w_uq[0].reshape(Q_LORA_RANK, N_HEADS, qk), ((0, 0), (0, 0), (0, HEAD_PAD - qk)))
    w_ukv0 = w_ukv[0].reshape(KV_LORA_RANK, N_HEADS, QK_NOPE_DIM + V_HEAD_DIM)
    w_k = jnp.pad(w_ukv0[:, :, :QK_NOPE_DIM], ((0, 0), (0, 0), (0, HEAD_PAD - QK_NOPE_DIM)))
    w_v = jnp.pad(w_ukv0[:, :, QK_NOPE_DIM:], ((0, 0), (0, 0), (0, HEAD_PAD - V_HEAD_DIM)))
    tables = {s: _rope_tables(s) for s in seq_lens}
    return {
        "ln_in_g": row(ln_in_g), "ln_in_b": row(ln_in_b),
        "w_in": w_in_p.astype(BF16),
        "w_pool": w_pool[0].astype(BF16), "pool_scale": row(pool_scale[0]),
        "q_norm_g": row(q_norm_g[0]), "w_uq": w_uq_p.reshape(Q_LORA_RANK, N_HEADS * HEAD_PAD).astype(BF16),
        "kv_norm_g": row(kv_norm_g[0]),
        "w_k": w_k.reshape(KV_LORA_RANK, N_HEADS * HEAD_PAD).astype(BF16),
        "w_vt": w_v.reshape(KV_LORA_RANK, N_HEADS * HEAD_PAD).T.astype(BF16),
        "w_o": w_o[0].astype(BF16),
        "ln1_g": row(ln1_g[0]), "ln1_b": row(ln1_b[0]),
        "w_router": jnp.stack([w_router[0].astype(BF16),
                               (w_router[0] - w_router[0].astype(BF16).astype(F32)).astype(BF16)]),
        "w_gate": w_gate[0].astype(BF16), "w_up": w_up[0].astype(BF16), "w_down": w_down[0].astype(BF16),
        "ln2_g": row(ln2_g[0]), "ln2_b": row(ln2_b[0]),
        "cos": {s: t[0] for s, t in tables.items()},
        "sin": {s: t[1] for s, t in tables.items()},
    }


def _trunk(x, p):
    b, s, d = x.shape
    t = b * s
    u, q, k, vt = _proj_call(x, p, tm=512)
    attn = _attn_call(q, k, vt, tq=512, tk=512)
    x1, x1b, aff = _mix_call(x, u, attn, p, tm=512)
    cap = max(1, (CAPACITY_FACTOR * t) // N_EXPERTS)
    gates_t = _select_call(aff.reshape(t, N_EXPERTS).T, cap)
    y = _moe_call(x1b.reshape(t, d), x1.reshape(t, d), gates_t.T, p, tm=1024)
    return y.reshape(b, s, d)


def kernel(x_prompt, x_sample, ln_in_g, ln_in_b, w_in, w_pool, pool_scale, q_norm_g, w_uq, kv_norm_g, w_ukv,
           w_o, ln1_g, ln1_b, w_router, w_gate, w_up, w_down, ln2_g, ln2_b):
    assert w_in.shape[0] == DEPTH == 1
    p = _prepare_params((x_prompt.shape[1], x_sample.shape[1]), ln_in_g, ln_in_b, w_in, w_pool, pool_scale,
                        q_norm_g, w_uq, kv_norm_g, w_ukv, w_o, ln1_g, ln1_b, w_router, w_gate, w_up, w_down,
                        ln2_g, ln2_b)
    return (_trunk(x_prompt, p), _trunk(x_sample, p))
```

```python
import functools
import math

import jax
import jax.numpy as jnp
from jax import lax
from jax.experimental import pallas as pl
from jax.experimental.pallas import tpu as pltpu

D_MODEL = 1024
POOL_WINDOWS = (2, 4, 8, 16)
POOL_DIM = 512
POOL_GROUP_DIM = 128
N_HEADS = 8
QK_NOPE_DIM = 64
QK_ROPE_DIM = 32
V_HEAD_DIM = 64
ATTN_DIM = N_HEADS * V_HEAD_DIM
Q_LORA_RANK = 384
KV_LORA_RANK = 256
ROPE_THETA = 10000.0
N_EXPERTS = 16
CAPACITY_FACTOR = 2
FFN_HIDDEN = 512
LN_EPS = 1e-5
RMS_EPS = 1e-6
DEPTH = 1
DEEPNORM_ALPHA = (2.0 * DEPTH) ** 0.25

LANES = 128
HEAD_PAD = 128
HEADS_PER_STEP = 2
MOE_TILE = 256
MOE_WIN = 64
ROW_ALIGN = 8
FFN_ROWS = 512
HALO = 8
VMEM_LIMIT = 56 * 1024 * 1024

BF16 = jnp.bfloat16
F32 = jnp.float32


def _layer_norm(x, g, b):
    mu = jnp.mean(x, axis=-1, keepdims=True)
    xc = x - mu
    var = jnp.mean(xc * xc, axis=-1, keepdims=True)
    return xc * lax.rsqrt(var + LN_EPS) * g + b


def _rms_norm(x, g):
    return x * lax.rsqrt(jnp.mean(x * x, axis=-1, keepdims=True) + RMS_EPS) * g


def _proj_kernel(x_ref, lng_ref, lnb_ref, win_ref, qg_ref, wuq_ref, kvg_ref, wk_ref, wvt_ref,
                 cos_ref, sin_ref, u_ref, q_ref, k_ref, vt_ref):
    xn = _layer_norm(x_ref[0], lng_ref[...], lnb_ref[...])
    proj = jnp.dot(xn.astype(BF16), win_ref[...], preferred_element_type=F32)
    o1 = POOL_DIM
    o2 = o1 + Q_LORA_RANK
    o3 = o2 + KV_LORA_RANK
    u_ref[0] = proj[:, :o1]
    qn = _rms_norm(proj[:, o1:o2], qg_ref[...])
    q = jnp.dot(qn.astype(BF16), wuq_ref[...], preferred_element_type=F32)
    kvn = _rms_norm(proj[:, o2:o3], kvg_ref[...]).astype(BF16)
    kk = jnp.dot(kvn, wk_ref[...], preferred_element_type=F32)
    vvt = lax.dot_general(wvt_ref[...], kvn, (((1,), (1,)), ((), ())),
                          preferred_element_type=F32)
    cos = cos_ref[...]
    sin = sin_ref[...]
    lane = lax.broadcasted_iota(jnp.int32, cos.shape, 1)
    first_half = lane < QK_NOPE_DIM + QK_ROPE_DIM // 2

    def rope(t):
        partner = jnp.where(first_half,
                            pltpu.roll(t, HEAD_PAD - QK_ROPE_DIM // 2, 1),
                            pltpu.roll(t, QK_ROPE_DIM // 2, 1))
        return t * cos + partner * sin

    k_rope = rope(proj[:, o3:o3 + HEAD_PAD])
    row = lax.broadcasted_iota(jnp.int32, (HEAD_PAD, vvt.shape[1]), 0)
    ones_rows = jnp.where(row >= V_HEAD_DIM, 1.0, 0.0).astype(F32)
    scale = math.log2(math.e) / math.sqrt(QK_NOPE_DIM + QK_ROPE_DIM)
    for h in range(N_HEADS):
        sl = slice(h * HEAD_PAD, (h + 1) * HEAD_PAD)
        q_ref[0, h] = (rope(q[:, sl]) * scale).astype(BF16)
        k_ref[0, h] = (kk[:, sl] + k_rope).astype(BF16)
        vt_ref[0, h] = (vvt[sl, :] + ones_rows).astype(BF16)


def _proj_call(x, p, tm):
    b, s, _ = x.shape
    full = lambda shape: pl.BlockSpec(shape, lambda i, j: (0,) * len(shape))
    hs = pl.BlockSpec((1, N_HEADS, tm, HEAD_PAD), lambda i, j: (i, 0, j, 0))
    return pl.pallas_call(
        _proj_kernel,
        grid=(b, s // tm),
        in_specs=[
            pl.BlockSpec((1, tm, D_MODEL), lambda i, j: (i, j, 0)),
            full((1, D_MODEL)), full((1, D_MODEL)),
            full(p["w_in"].shape),
            full((1, Q_LORA_RANK)), full(p["w_uq"].shape),
            full((1, KV_LORA_RANK)), full(p["w_k"].shape), full(p["w_vt"].shape),
            pl.BlockSpec((tm, HEAD_PAD), lambda i, j: (j, 0)),
            pl.BlockSpec((tm, HEAD_PAD), lambda i, j: (j, 0)),
        ],
        out_specs=[
            pl.BlockSpec((1, tm, POOL_DIM), lambda i, j: (i, j, 0)),
            hs, hs, pl.BlockSpec((1, N_HEADS, HEAD_PAD, tm), lambda i, j: (i, 0, 0, j)),
        ],
        out_shape=[
            jax.ShapeDtypeStruct((b, s, POOL_DIM), F32),
            jax.ShapeDtypeStruct((b, N_HEADS, s, HEAD_PAD), BF16),
            jax.ShapeDtypeStruct((b, N_HEADS, s, HEAD_PAD), BF16),
            jax.ShapeDtypeStruct((b, N_HEADS, HEAD_PAD, s), BF16),
        ],
        compiler_params=pltpu.CompilerParams(
            dimension_semantics=("arbitrary", "arbitrary"), vmem_limit_bytes=VMEM_LIMIT),
        name="proj",
    )(x, p["ln_in_g"], p["ln_in_b"], p["w_in"], p["q_norm_g"], p["w_uq"], p["kv_norm_g"],
      p["w_k"], p["w_vt"], p["cos"][s], p["sin"][s])


def _attn_kernel(q_ref, k_ref, vt_ref, o_ref, m_ref, acc_ref, s_ref, p_ref, alpha_ref, *, tk):
    s_len = k_ref.shape[2]
    tq = q_ref.shape[2]
    n_chunks = s_len // tk
    m_ref[...] = jnp.full(m_ref.shape, -jnp.inf, F32)
    acc_ref[...] = jnp.zeros(acc_ref.shape, F32)

    def scores(j, slot):
        start = j * tk
        for hh in range(HEADS_PER_STEP):
            k = k_ref[0, hh, pl.ds(start, tk), :]
            s_ref[slot, hh] = lax.dot_general(k, q_ref[0, hh], (((1,), (1,)), ((), ())),
                                              preferred_element_type=F32)

    def softmax(slot):
        for hh in range(HEADS_PER_STEP):
            st = s_ref[slot, hh]
            m_old = m_ref[hh]
            m_new = jnp.maximum(m_old, jnp.max(st, axis=0, keepdims=True))
            p_ref[slot, hh] = jnp.exp2(st - m_new).astype(BF16)
            alpha_ref[slot, hh] = jnp.exp2(m_old - m_new)
            m_ref[hh] = m_new

    def values(j, slot):
        start = j * tk
        for hh in range(HEADS_PER_STEP):
            vt = vt_ref[0, hh, :, pl.ds(start, tk)]
            acc_ref[hh] = acc_ref[hh] * alpha_ref[slot, hh] + jnp.dot(
                vt, p_ref[slot, hh], preferred_element_type=F32)

    scores(0, 0)
    for j in range(n_chunks):
        softmax(j % 2)
        if j + 1 < n_chunks:
            scores(j + 1, (j + 1) % 2)
        if j >= 1:
            values(j - 1, (j - 1) % 2)
    values(n_chunks - 1, (n_chunks - 1) % 2)
    heads = []
    for hh in range(HEADS_PER_STEP):
        acc = acc_ref[hh]
        heads.append((acc / acc[V_HEAD_DIM:V_HEAD_DIM + 1, :]).T)
    lane = lax.broadcasted_iota(jnp.int32, (tq, HEAD_PAD), 1)
    out = jnp.where(lane < V_HEAD_DIM, heads[0], pltpu.roll(heads[1], V_HEAD_DIM, 1))
    o_ref[0] = out.astype(BF16)


def _attn_call(q, k, vt, tq, tk):
    b, h, s, _ = q.shape
    hp = HEADS_PER_STEP
    return pl.pallas_call(
        functools.partial(_attn_kernel, tk=tk),
        grid=(b, h // hp, s // tq),
        in_specs=[
            pl.BlockSpec((1, hp, tq, HEAD_PAD), lambda i, j, t: (i, j, t, 0)),
            pl.BlockSpec((1, hp, s, HEAD_PAD), lambda i, j, t: (i, j, 0, 0)),
            pl.BlockSpec((1, hp, HEAD_PAD, s), lambda i, j, t: (i, j, 0, 0)),
        ],
        out_specs=pl.BlockSpec((1, tq, hp * V_HEAD_DIM), lambda i, j, t: (i, t, j)),
        out_shape=jax.ShapeDtypeStruct((b, s, ATTN_DIM), BF16),
        scratch_shapes=[
            pltpu.VMEM((hp, 1, tq), F32),
            pltpu.VMEM((hp, HEAD_PAD, tq), F32),
            pltpu.VMEM((2, hp, tk, tq), F32),
            pltpu.VMEM((2, hp, tk, tq), BF16),
            pltpu.VMEM((2, hp, 1, tq), F32),
        ],
        compiler_params=pltpu.CompilerParams(
            dimension_semantics=("arbitrary", "arbitrary", "arbitrary"), vmem_limit_bytes=VMEM_LIMIT),
        name="attn",
    )(q, k, vt)


def _mix_kernel(x_ref, up_ref, um_ref, un_ref, at_ref, lng_ref, lnb_ref, wp_ref, ps_ref, wo_ref,
                g1_ref, b1_ref, wr_ref, x1_ref, x1b_ref, aff_ref, ubuf):
    tm = um_ref.shape[1]
    j = pl.program_id(1)
    n_tiles = pl.num_programs(1)
    s_len = n_tiles * tm
    xn = _layer_norm(x_ref[0], lng_ref[...], lnb_ref[...])

    ubuf[pl.ds(0, HALO), :] = jnp.where(j > 0, up_ref[0], 0.0)
    ubuf[pl.ds(HALO, tm), :] = um_ref[0]
    ubuf[pl.ds(HALO + tm, HALO), :] = jnp.where(j < n_tiles - 1, un_ref[0], 0.0)

    pos = j * tm + lax.broadcasted_iota(jnp.int32, (tm, 1), 0)
    mix = jnp.dot(at_ref[0], wo_ref[pl.ds(POOL_DIM, ATTN_DIM), :], preferred_element_type=F32)
    for g, w in enumerate(POOL_WINDOWS):
        cols = pl.ds(g * POOL_GROUP_DIM, POOL_GROUP_DIM)
        win_sum = ubuf[pl.ds(HALO - w // 2, tm), cols]
        for off in range(-w // 2 + 1, w // 2):
            win_sum = win_sum + ubuf[pl.ds(HALO + off, tm), cols]
        lo = jnp.maximum(pos - w // 2, 0)
        hi = jnp.minimum(pos + w // 2 - 1, s_len - 1)
        cnt = (hi - lo + 1).astype(F32)
        d = win_sum / cnt - ubuf[pl.ds(HALO, tm), cols]
        y = jnp.dot(d.astype(BF16), wp_ref[g], preferred_element_type=F32) * ps_ref[:, cols]
        mix = mix + jnp.dot(y.astype(BF16), wo_ref[cols, :], preferred_element_type=F32)

    x1 = _layer_norm(DEEPNORM_ALPHA * xn + mix, g1_ref[...], b1_ref[...])
    x1_ref[0] = x1
    x1_hi = x1.astype(BF16)
    x1b_ref[0] = x1_hi
    x1_lo = (x1 - x1_hi.astype(F32)).astype(BF16)
    logits = (jnp.dot(x1_hi, wr_ref[0], preferred_element_type=F32)
              + jnp.dot(x1_hi, wr_ref[1], preferred_element_type=F32)
              + jnp.dot(x1_lo, wr_ref[0], preferred_element_type=F32))
    e = jnp.exp(logits - jnp.max(logits, axis=-1, keepdims=True))
    aff_ref[0] = e / jnp.sum(e, axis=-1, keepdims=True)


def _mix_call(x, u, attn, p, tm):
    b, s, _ = x.shape
    full = lambda shape: pl.BlockSpec(shape, lambda i, j: (0,) * len(shape))
    tile = lambda width: pl.BlockSpec((1, tm, width), lambda i, j: (i, j, 0))
    hb = tm // HALO
    last_halo_block = s // HALO - 1
    return pl.pallas_call(
        _mix_kernel,
        grid=(b, s // tm),
        in_specs=[
            tile(D_MODEL),
            pl.BlockSpec((1, HALO, POOL_DIM), lambda i, j: (i, jnp.maximum(j * hb - 1, 0), 0)),
            tile(POOL_DIM),
            pl.BlockSpec((1, HALO, POOL_DIM), lambda i, j: (i, jnp.minimum((j + 1) * hb, last_halo_block), 0)),
            tile(ATTN_DIM),
            full((1, D_MODEL)), full((1, D_MODEL)),
            full(p["w_pool"].shape), full((1, POOL_DIM)), full(p["w_o"].shape),
            full((1, D_MODEL)), full((1, D_MODEL)), full(p["w_router"].shape),
        ],
        out_specs=[tile(D_MODEL), tile(D_MODEL), tile(N_EXPERTS)],
        out_shape=[
            jax.ShapeDtypeStruct((b, s, D_MODEL), F32),
            jax.ShapeDtypeStruct((b, s, D_MODEL), BF16),
            jax.ShapeDtypeStruct((b, s, N_EXPERTS), F32),
        ],
        scratch_shapes=[pltpu.VMEM((tm + 2 * HALO, POOL_DIM), F32)],
        compiler_params=pltpu.CompilerParams(
            dimension_semantics=("arbitrary", "arbitrary"), vmem_limit_bytes=VMEM_LIMIT),
        name="mix",
    )(x, u, u, u, attn, p["ln_in_g"], p["ln_in_b"], p["w_pool"], p["pool_scale"], p["w_o"],
      p["ln1_g"], p["ln1_b"], p["w_router"])


def _select_kernel(aff_ref, gate_ref, rows_ref, offs_ref, *, cap, chunk):
    n_exp, t = aff_ref.shape
    n_chunks = t // chunk

    def count(pred):
        def body(c, acc):
            start = pl.multiple_of(c * chunk, chunk)
            bits = lax.bitcast_convert_type(aff_ref[:, pl.ds(start, chunk)], jnp.int32)
            tok = start + lax.broadcasted_iota(jnp.int32, (n_exp, chunk), 1)
            return acc + jnp.sum(jnp.where(pred(bits, tok), 1.0, 0.0), axis=1, keepdims=True)
        return lax.fori_loop(0, n_chunks, body, jnp.zeros((n_exp, 1), F32))

    def value_step(i, thr):
        cand = thr | (1 << (30 - i))
        n_ge = count(lambda bits, tok: bits >= cand)
        return jnp.where(n_ge >= cap, cand, thr)
    thr = lax.fori_loop(0, 31, value_step, jnp.zeros((n_exp, 1), jnp.int32))

    need = cap - count(lambda bits, tok: bits > thr)
    index_bits = t.bit_length()

    def index_step(i, below):
        cand = below + (1 << (index_bits - 1 - i))
        n_ties = count(lambda bits, tok: (bits == thr) & (tok < cand))
        return jnp.where((cand <= t) & (n_ties < need), cand, below)
    cut = lax.fori_loop(0, index_bits, index_step, jnp.zeros((n_exp, 1), jnp.int32)) + 1

    def write(c, counts):
        start = pl.multiple_of(c * chunk, chunk)
        aff = aff_ref[:, pl.ds(start, chunk)]
        bits = lax.bitcast_convert_type(aff, jnp.int32)
        tok = start + lax.broadcasted_iota(jnp.int32, (n_exp, chunk), 1)
        sel = (bits > thr) | ((bits == thr) & (tok < cut))
        gate = jnp.where(sel, aff, 0.0)
        gate_ref[:, pl.ds(start, chunk)] = gate
        tile_lane = lax.broadcasted_iota(jnp.int32, counts.shape, 1)
        for k in range(chunk // MOE_TILE):
            n = jnp.sum(jnp.where(gate[:, k * MOE_TILE:(k + 1) * MOE_TILE] > 0.0, 1.0, 0.0),
                        axis=1, keepdims=True)
            counts = jnp.where(tile_lane == c * (chunk // MOE_TILE) + k, n, counts)
        return counts
    n_tiles = t // MOE_TILE
    counts = lax.fori_loop(0, n_chunks, write, jnp.zeros((n_exp, n_tiles), F32))

    rows = jnp.floor((counts + (ROW_ALIGN - 1)) * (1.0 / ROW_ALIGN)) * ROW_ALIGN
    before = (lax.broadcasted_iota(jnp.int32, (n_tiles, n_tiles), 0)
              < lax.broadcasted_iota(jnp.int32, (n_tiles, n_tiles), 1))
    offs = jnp.dot(rows.astype(BF16), jnp.where(before, 1.0, 0.0).astype(BF16), preferred_element_type=F32)
    rows_ref[...] = rows.astype(jnp.int32)
    offs_ref[...] = offs.astype(jnp.int32)


def _select_call(aff_t, cap):
    n_exp, t = aff_t.shape
    n_tiles = t // MOE_TILE
    return pl.pallas_call(
        functools.partial(_select_kernel, cap=cap, chunk=2048),
        out_shape=[
            jax.ShapeDtypeStruct((n_exp, t), F32),
            jax.ShapeDtypeStruct((n_exp, n_tiles), jnp.int32),
            jax.ShapeDtypeStruct((n_exp, n_tiles), jnp.int32),
        ],
        compiler_params=pltpu.CompilerParams(vmem_limit_bytes=VMEM_LIMIT),
        name="select",
    )(aff_t)


def _pack_bf16_pair(x):
    half = x.shape[1] // 2
    hi = lax.bitcast_convert_type(x[:, :half], jnp.uint32) & jnp.uint32(0xFFFF0000)
    lo = lax.shift_right_logical(lax.bitcast_convert_type(x[:, half:], jnp.uint32), jnp.uint32(16))
    return hi | lo


def _unpack_bf16_pair(u):
    a = lax.bitcast_convert_type(u & jnp.uint32(0xFFFF0000), F32)
    b = lax.bitcast_convert_type(lax.shift_left(u, jnp.uint32(16)), F32)
    return a.astype(BF16), b.astype(BF16)


def _scalar_max(ref, i):
    m = ref[0, i]
    for e in range(1, N_EXPERTS):
        m = jnp.maximum(m, ref[e, i])
    return m


def _window_copies(ref_of, rows_ref, offs_ref, i, p, buf, sem, to_hbm, only_if_needed):
    copies = []
    for e in range(N_EXPERTS):
        start = pl.multiple_of(offs_ref[e, i] + p * MOE_WIN, ROW_ALIGN)
        hbm = ref_of.at[e, pl.ds(start, MOE_WIN)]
        vmem = buf.at[pl.ds(e * MOE_WIN, MOE_WIN)]
        copy = pltpu.make_async_copy(vmem, hbm, sem) if to_hbm else pltpu.make_async_copy(hbm, vmem, sem)
        needed = rows_ref[e, i] > p * MOE_WIN
        copies.append((copy, needed if only_if_needed else None))
    return copies


def _start_all(copies):
    for copy, cond in copies:
        if cond is None:
            copy.start()
        else:
            pl.when(cond)(copy.start)


def _wait_all(copies):
    for copy, cond in copies:
        if cond is None:
            copy.wait()
        else:
            pl.when(cond)(copy.wait)


def _dispatch_kernel(rows_ref, offs_ref, gt_ref, xb_ref, xe_ref, stack_ref, zero_ref, sem, zero_sem):
    i = pl.program_id(0)
    n_tiles = pl.num_programs(0)
    slot = i % 2
    n_rows = N_EXPERTS * MOE_WIN
    sel_t = jnp.where(gt_ref[...] > 0.0, 1.0, 0.0).astype(BF16)
    earlier = (lax.broadcasted_iota(jnp.int32, (MOE_TILE, MOE_TILE), 0)
               < lax.broadcasted_iota(jnp.int32, (MOE_TILE, MOE_TILE), 1))
    rank_t = jnp.dot(sel_t, jnp.where(earlier, 1.0, 0.0).astype(BF16), preferred_element_type=F32)
    spread = jnp.where(lax.broadcasted_iota(jnp.int32, (n_rows, N_EXPERTS), 0) // MOE_WIN
                       == lax.broadcasted_iota(jnp.int32, (n_rows, N_EXPERTS), 1), 1.0, 0.0).astype(BF16)
    rank_rows = jnp.dot(spread, rank_t.astype(BF16), preferred_element_type=F32)
    sel_rows = jnp.dot(spread, sel_t, preferred_element_type=F32)
    slot_in_win = lax.broadcasted_iota(jnp.int32, (n_rows, 1), 0) % MOE_WIN

    def fill(p):
        want = (slot_in_win + p * MOE_WIN).astype(F32)
        onehot = jnp.where((rank_rows == want) & (sel_rows > 0.0), 1.0, 0.0).astype(BF16)
        picked = jnp.dot(onehot, xb_ref[...], preferred_element_type=F32)
        stack_ref[slot] = _pack_bf16_pair(picked)

    copies = lambda tile, p, s: _window_copies(xe_ref, rows_ref, offs_ref, tile, p, stack_ref.at[s],
                                               sem.at[s], True, True)
    passes = lambda tile: (_scalar_max(rows_ref, tile) + MOE_WIN - 1) // MOE_WIN

    fill(0)
    prev = jnp.maximum(i - 1, 0)

    @pl.when((i > 0) & (passes(prev) <= 1))
    def _():
        _wait_all(copies(prev, 0, 1 - slot))

    _start_all(copies(i, 0, slot))
    n_pass = passes(i)

    @pl.when(n_pass > 1)
    def _():
        _wait_all(copies(i, 0, slot))

        def extra(p, carry):
            fill(p)
            _start_all(copies(i, p, slot))
            _wait_all(copies(i, p, slot))
            return carry
        lax.fori_loop(1, n_pass, extra, 0)

    @pl.when((i == n_tiles - 1) & (n_pass <= 1))
    def _():
        _wait_all(copies(i, 0, slot))

    @pl.when(i == n_tiles - 1)
    def _():
        cap_rows = xe_ref.shape[1]
        zero_ref[...] = jnp.zeros(zero_ref.shape, jnp.uint32)
        for e in range(N_EXPERTS):
            total = offs_ref[e, i] + rows_ref[e, i]
            n_full = (cap_rows - total) // MOE_WIN

            def zero_copy(start, e=e):
                return pltpu.make_async_copy(
                    zero_ref, xe_ref.at[e, pl.ds(pl.multiple_of(start, ROW_ALIGN), MOE_WIN)], zero_sem)

            def fill_start(k, carry, total=total, zero_copy=zero_copy):
                zero_copy(total + k * MOE_WIN).start()
                return carry

            def fill_wait(k, carry, zero_copy=zero_copy):
                zero_copy(0).wait()
                return carry

            lax.fori_loop(0, n_full, fill_start, 0)
            lax.fori_loop(0, n_full, fill_wait, 0)
            zero_copy(cap_rows - MOE_WIN).start()
            zero_copy(cap_rows - MOE_WIN).wait()


def _dispatch_call(rows, offs, gates_t, x1b, cap_rows):
    n_exp, t = gates_t.shape
    n_tiles = t // MOE_TILE
    grid_spec = pltpu.PrefetchScalarGridSpec(
        num_scalar_prefetch=2,
        grid=(n_tiles,),
        in_specs=[
            pl.BlockSpec((n_exp, MOE_TILE), lambda i, rows, offs: (0, i)),
            pl.BlockSpec((MOE_TILE, D_MODEL), lambda i, rows, offs: (i, 0)),
        ],
        out_specs=pl.BlockSpec(memory_space=pl.ANY),
        scratch_shapes=[
            pltpu.VMEM((2, n_exp * MOE_WIN, D_MODEL // 2), jnp.uint32),
            pltpu.VMEM((MOE_WIN, D_MODEL // 2), jnp.uint32),
            pltpu.SemaphoreType.DMA((2,)),
            pltpu.SemaphoreType.DMA(()),
        ],
    )
    return pl.pallas_call(
        _dispatch_kernel,
        grid_spec=grid_spec,
        out_shape=jax.ShapeDtypeStruct((n_exp, cap_rows, D_MODEL // 2), jnp.uint32),
        compiler_params=pltpu.CompilerParams(
            dimension_semantics=("arbitrary",), vmem_limit_bytes=VMEM_LIMIT),
        name="dispatch",
    )(rows, offs, gates_t, x1b)


def _ffn_kernel(total_ref, xe_ref, wg_ref, wu_ref, wd_ref, ye_ref):
    e = pl.program_id(0)
    blk = pl.program_id(1)
    n_blk_rows = xe_ref.shape[1]
    total = total_ref[e]

    @pl.when(blk * n_blk_rows < total)
    def _():
        xa, xb = _unpack_bf16_pair(xe_ref[0])
        half = D_MODEL // 2
        hg = (jnp.dot(xa, wg_ref[0, :half, :], preferred_element_type=F32)
              + jnp.dot(xb, wg_ref[0, half:, :], preferred_element_type=F32))
        hu = (jnp.dot(xa, wu_ref[0, :half, :], preferred_element_type=F32)
              + jnp.dot(xb, wu_ref[0, half:, :], preferred_element_type=F32))
        h = hg * jax.nn.sigmoid(hg) * hu
        ye = jnp.dot(h.astype(BF16), wd_ref[0], preferred_element_type=F32)
        ye_ref[0] = _pack_bf16_pair(ye.astype(BF16).astype(F32))

    @pl.when(blk * n_blk_rows >= total)
    def _():
        ye_ref[...] = jnp.zeros(ye_ref.shape, jnp.uint32)


def _ffn_call(totals, xe, p):
    n_exp, cap_rows, half = xe.shape
    n_blk = cap_rows // FFN_ROWS

    def xe_index(e, b, totals):
        last = jnp.maximum((totals[e] + FFN_ROWS - 1) // FFN_ROWS - 1, 0)
        return (e, jnp.minimum(b, last), 0)

    grid_spec = pltpu.PrefetchScalarGridSpec(
        num_scalar_prefetch=1,
        grid=(n_exp, n_blk),
        in_specs=[
            pl.BlockSpec((1, FFN_ROWS, half), xe_index),
            pl.BlockSpec((1, D_MODEL, FFN_HIDDEN), lambda e, b, totals: (e, 0, 0)),
            pl.BlockSpec((1, D_MODEL, FFN_HIDDEN), lambda e, b, totals: (e, 0, 0)),
            pl.BlockSpec((1, FFN_HIDDEN, D_MODEL), lambda e, b, totals: (e, 0, 0)),
        ],
        out_specs=pl.BlockSpec((1, FFN_ROWS, half), lambda e, b, totals: (e, b, 0)),
    )
    return pl.pallas_call(
        _ffn_kernel,
        grid_spec=grid_spec,
        out_shape=jax.ShapeDtypeStruct(xe.shape, jnp.uint32),
        compiler_params=pltpu.CompilerParams(
            dimension_semantics=("arbitrary", "arbitrary"), vmem_limit_bytes=VMEM_LIMIT),
        name="ffn",
    )(totals, xe, p["w_gate"], p["w_up"], p["w_down"])


def _combine_kernel(rows_ref, offs_ref, g_ref, x1_ref, g2_ref, b2_ref, ye_ref, o_ref, land_ref, moe_ref, sem):
    i = pl.program_id(0)
    n_tiles = pl.num_programs(0)
    slot = i % 2
    n_rows = N_EXPERTS * MOE_WIN
    half = D_MODEL // 2
    fetch = lambda tile, p, s, needed: _window_copies(ye_ref, rows_ref, offs_ref, tile, p, land_ref.at[s],
                                                      sem.at[s], False, needed)

    @pl.when(i == 0)
    def _():
        _start_all(fetch(0, 0, 0, False))

    @pl.when(i + 1 < n_tiles)
    def _():
        _start_all(fetch(jnp.minimum(i + 1, n_tiles - 1), 0, 1 - slot, False))

    gates = g_ref[...]
    sel = jnp.where(gates > 0.0, 1.0, 0.0).astype(BF16)
    earlier = (lax.broadcasted_iota(jnp.int32, (MOE_TILE, MOE_TILE), 1)
               < lax.broadcasted_iota(jnp.int32, (MOE_TILE, MOE_TILE), 0))
    rank = jnp.dot(jnp.where(earlier, 1.0, 0.0).astype(BF16), sel, preferred_element_type=F32)
    spread = jnp.where(lax.broadcasted_iota(jnp.int32, (N_EXPERTS, n_rows), 1) // MOE_WIN
                       == lax.broadcasted_iota(jnp.int32, (N_EXPERTS, n_rows), 0), 1.0, 0.0).astype(BF16)
    rank_cols = jnp.dot(rank.astype(BF16), spread, preferred_element_type=F32)
    gate_cols = jnp.dot(gates.astype(BF16), spread, preferred_element_type=F32)
    slot_in_win = lax.broadcasted_iota(jnp.int32, (1, n_rows), 1) % MOE_WIN
    row_expert = lax.broadcasted_iota(jnp.int32, (n_rows, 1), 0) // MOE_WIN
    row_in_win = lax.broadcasted_iota(jnp.int32, (n_rows, 1), 0) % MOE_WIN
    seg_rows = jnp.zeros((n_rows, 1), jnp.int32)
    for e in range(N_EXPERTS):
        seg_rows = jnp.where(row_expert == e, rows_ref[e, i], seg_rows)

    def gather(p, s):
        want = (slot_in_win + p * MOE_WIN).astype(F32)
        weights = jnp.where(rank_cols == want, gate_cols, 0.0).astype(BF16)
        landed = jnp.where(row_in_win + p * MOE_WIN < seg_rows, land_ref[s], jnp.uint32(0))
        ya, yb = _unpack_bf16_pair(landed)
        return (jnp.dot(weights, ya, preferred_element_type=F32),
                jnp.dot(weights, yb, preferred_element_type=F32))

    _wait_all(fetch(i, 0, slot, False))
    ma, mb = gather(0, slot)
    moe_ref[:, :half] = ma
    moe_ref[:, half:] = mb
    n_pass = (_scalar_max(rows_ref, i) + MOE_WIN - 1) // MOE_WIN

    @pl.when(n_pass > 1)
    def _():
        def extra(p, carry):
            _start_all(fetch(i, p, 2, True))
            _wait_all(fetch(i, p, 2, True))
            ea, eb = gather(p, 2)
            moe_ref[:, :half] += ea
            moe_ref[:, half:] += eb
            return carry
        lax.fori_loop(1, n_pass, extra, 0)

    o_ref[...] = _layer_norm(DEEPNORM_ALPHA * x1_ref[...] + moe_ref[...], g2_ref[...], b2_ref[...])


def _combine_call(rows, offs, gates, x1, ye, p):
    t, n_exp = gates.shape
    n_tiles = t // MOE_TILE
    full = lambda shape: pl.BlockSpec(shape, lambda i, rows, offs: (0,) * len(shape))
    grid_spec = pltpu.PrefetchScalarGridSpec(
        num_scalar_prefetch=2,
        grid=(n_tiles,),
        in_specs=[
            pl.BlockSpec((MOE_TILE, n_exp), lambda i, rows, offs: (i, 0)),
            pl.BlockSpec((MOE_TILE, D_MODEL), lambda i, rows, offs: (i, 0)),
            full((1, D_MODEL)), full((1, D_MODEL)),
            pl.BlockSpec(memory_space=pl.ANY),
        ],
        out_specs=pl.BlockSpec((MOE_TILE, D_MODEL), lambda i, rows, offs: (i, 0)),
        scratch_shapes=[
            pltpu.VMEM((3, n_exp * MOE_WIN, D_MODEL // 2), jnp.uint32),
            pltpu.VMEM((MOE_TILE, D_MODEL), F32),
            pltpu.SemaphoreType.DMA((3,)),
        ],
    )
    return pl.pallas_call(
        _combine_kernel,
        grid_spec=grid_spec,
        out_shape=jax.ShapeDtypeStruct((t, D_MODEL), F32),
        compiler_params=pltpu.CompilerParams(
            dimension_semantics=("arbitrary",), vmem_limit_bytes=VMEM_LIMIT),
        name="combine",
    )(rows, offs, gates, x1, p["ln2_g"], p["ln2_b"], ye)


def _rope_tables(s):
    half = QK_ROPE_DIM // 2
    inv = 1.0 / (ROPE_THETA ** (jnp.arange(0, QK_ROPE_DIM, 2, dtype=F32) / QK_ROPE_DIM))
    ang = jnp.arange(s, dtype=F32)[:, None] * inv[None, :]
    cos, sin = jnp.cos(ang), jnp.sin(ang)
    tail = HEAD_PAD - QK_NOPE_DIM - QK_ROPE_DIM
    cos_t = jnp.concatenate([jnp.ones((s, QK_NOPE_DIM), F32), cos, cos, jnp.ones((s, tail), F32)], axis=1)
    sin_t = jnp.concatenate([jnp.zeros((s, QK_NOPE_DIM), F32), -sin, sin, jnp.zeros((s, tail), F32)], axis=1)
    assert half * 2 == QK_ROPE_DIM
    return cos_t, sin_t


def _prepare_params(seq_lens, ln_in_g, ln_in_b, w_in, w_pool, pool_scale, q_norm_g, w_uq, kv_norm_g, w_ukv,
                    w_o, ln1_g, ln1_b, w_router, w_gate, w_up, w_down, ln2_g, ln2_b):
    row = lambda a: a.reshape(1, -1).astype(F32)
    o3 = POOL_DIM + Q_LORA_RANK + KV_LORA_RANK
    w_in0 = w_in[0]
    w_in_p = jnp.concatenate([
        w_in0[:, :o3], jnp.zeros((D_MODEL, QK_NOPE_DIM), F32), w_in0[:, o3:],
        jnp.zeros((D_MODEL, HEAD_PAD - QK_NOPE_DIM - QK_ROPE_DIM), F32)], axis=1)
    qk = QK_NOPE_DIM + QK_ROPE_DIM
    w_uq_p = jnp.pad(w_uq[0].reshape(Q_LORA_RANK, N_HEADS, qk), ((0, 0), (0, 0), (0, HEAD_PAD - qk)))
    w_ukv0 = w_ukv[0].reshape(KV_LORA_RANK, N_HEADS, QK_NOPE_DIM + V_HEAD_DIM)
    w_k = jnp.pad(w_ukv0[:, :, :QK_NOPE_DIM], ((0, 0), (0, 0), (0, HEAD_PAD - QK_NOPE_DIM)))
    w_v = jnp.pad(w_ukv0[:, :, QK_NOPE_DIM:], ((0, 0), (0, 0), (0, HEAD_PAD - V_HEAD_DIM)))
    tables = {s: _rope_tables(s) for s in seq_lens}
    return {
        "ln_in_g": row(ln_in_g), "ln_in_b": row(ln_in_b),
        "w_in": w_in_p.astype(BF16),
        "w_pool": w_pool[0].astype(BF16), "pool_scale": row(pool_scale[0]),
        "q_norm_g": row(q_norm_g[0]), "w_uq": w_uq_p.reshape(Q_LORA_RANK, N_HEADS * HEAD_PAD).astype(BF16),
        "kv_norm_g": row(kv_norm_g[0]),
        "w_k": w_k.reshape(KV_LORA_RANK, N_HEADS * HEAD_PAD).astype(BF16),
        "w_vt": w_v.reshape(KV_LORA_RANK, N_HEADS * HEAD_PAD).T.astype(BF16),
        "w_o": w_o[0].astype(BF16),
        "ln1_g": row(ln1_g[0]), "ln1_b": row(ln1_b[0]),
        "w_router": jnp.stack([w_router[0].astype(BF16),
                               (w_router[0] - w_router[0].astype(BF16).astype(F32)).astype(BF16)]),
        "w_gate": w_gate[0].astype(BF16), "w_up": w_up[0].astype(BF16), "w_down": w_down[0].astype(BF16),
        "ln2_g": row(ln2_g[0]), "ln2_b": row(ln2_b[0]),
        "cos": {s: t[0] for s, t in tables.items()},
        "sin": {s: t[1] for s, t in tables.items()},
    }


def _trunk(x, p):
    b, s, d = x.shape
    t = b * s
    u, q, k, vt = _proj_call(x, p, tm=512)
    attn = _attn_call(q, k, vt, tq=512, tk=512)
    x1, x1b, aff = _mix_call(x, u, attn, p, tm=512)
    cap = max(1, (CAPACITY_FACTOR * t) // N_EXPERTS)
    gates_t, rows, offs = _select_call(aff.reshape(t, N_EXPERTS).T, cap)
    cap_rows = -(-(cap + ROW_ALIGN * (t // MOE_TILE) + MOE_WIN) // FFN_ROWS) * FFN_ROWS
    xe = _dispatch_call(rows, offs, gates_t, x1b.reshape(t, d), cap_rows)
    totals = offs[:, -1] + rows[:, -1]
    ye = _ffn_call(totals, xe, p)
    y = _combine_call(rows, offs, gates_t.T, x1.reshape(t, d), ye, p)
    return y.reshape(b, s, d)


def kernel(x_prompt, x_sample, ln_in_g, ln_in_b, w_in, w_pool, pool_scale, q_norm_g, w_uq, kv_norm_g, w_ukv,
           w_o, ln1_g, ln1_b, w_router, w_gate, w_up, w_down, ln2_g, ln2_b):
    assert w_in.shape[0] == DEPTH == 1
    p = _prepare_params((x_prompt.shape[1], x_sample.shape[1]), ln_in_g, ln_in_b, w_in, w_pool, pool_scale,
                        q_norm_g, w_uq, kv_norm_g, w_ukv, w_o, ln1_g, ln1_b, w_router, w_gate, w_up, w_down,
                        ln2_g, ln2_b)
    return (_trunk(x_prompt, p), _trunk(x_sample, p))
```

```python
import functools
import math

import jax
import jax.numpy as jnp
from jax import lax
from jax.experimental import pallas as pl
from jax.experimental.pallas import tpu as pltpu

D_MODEL = 1024
POOL_WINDOWS = (2, 4, 8, 16)
POOL_DIM = 512
POOL_GROUP_DIM = 128
N_HEADS = 8
QK_NOPE_DIM = 64
QK_ROPE_DIM = 32
V_HEAD_DIM = 64
ATTN_DIM = N_HEADS * V_HEAD_DIM
Q_LORA_RANK = 384
KV_LORA_RANK = 256
ROPE_THETA = 10000.0
N_EXPERTS = 16
CAPACITY_FACTOR = 2
FFN_HIDDEN = 512
LN_EPS = 1e-5
RMS_EPS = 1e-6
DEPTH = 1
DEEPNORM_ALPHA = (2.0 * DEPTH) ** 0.25

LANES = 128
HEAD_PAD = 128
HEADS_PER_STEP = 2
MOE_TILE = 256
MOE_WIN = 64
ROW_ALIGN = 8
FFN_ROWS = 512
HALO = 8
VMEM_LIMIT = 56 * 1024 * 1024

BF16 = jnp.bfloat16
F32 = jnp.float32


def _layer_norm(x, g, b):
    mu = jnp.mean(x, axis=-1, keepdims=True)
    xc = x - mu
    var = jnp.mean(xc * xc, axis=-1, keepdims=True)
    return xc * lax.rsqrt(var + LN_EPS) * g + b


def _rms_norm(x, g):
    return x * lax.rsqrt(jnp.mean(x * x, axis=-1, keepdims=True) + RMS_EPS) * g


def _proj_kernel(x_ref, lng_ref, lnb_ref, win_ref, qg_ref, wuq_ref, kvg_ref, wk_ref, wvt_ref,
                 cos_ref, sin_ref, u_ref, q_ref, k_ref, vt_ref):
    xn = _layer_norm(x_ref[0], lng_ref[...], lnb_ref[...])
    proj = jnp.dot(xn.astype(BF16), win_ref[...], preferred_element_type=F32)
    o1 = POOL_DIM
    o2 = o1 + Q_LORA_RANK
    o3 = o2 + KV_LORA_RANK
    u_ref[0] = proj[:, :o1]
    qn = _rms_norm(proj[:, o1:o2], qg_ref[...])
    q = jnp.dot(qn.astype(BF16), wuq_ref[...], preferred_element_type=F32)
    kvn = _rms_norm(proj[:, o2:o3], kvg_ref[...]).astype(BF16)
    kk = jnp.dot(kvn, wk_ref[...], preferred_element_type=F32)
    vvt = lax.dot_general(wvt_ref[...], kvn, (((1,), (1,)), ((), ())),
                          preferred_element_type=F32)
    cos = cos_ref[...]
    sin = sin_ref[...]
    lane = lax.broadcasted_iota(jnp.int32, cos.shape, 1)
    first_half = lane < QK_NOPE_DIM + QK_ROPE_DIM // 2

    def rope(t):
        partner = jnp.where(first_half,
                            pltpu.roll(t, HEAD_PAD - QK_ROPE_DIM // 2, 1),
                            pltpu.roll(t, QK_ROPE_DIM // 2, 1))
        return t * cos + partner * sin

    k_rope = rope(proj[:, o3:o3 + HEAD_PAD])
    row = lax.broadcasted_iota(jnp.int32, (HEAD_PAD, vvt.shape[1]), 0)
    ones_rows = jnp.where(row >= V_HEAD_DIM, 1.0, 0.0).astype(F32)
    scale = math.log2(math.e) / math.sqrt(QK_NOPE_DIM + QK_ROPE_DIM)
    for h in range(N_HEADS):
        sl = slice(h * HEAD_PAD, (h + 1) * HEAD_PAD)
        q_ref[0, h] = (rope(q[:, sl]) * scale).astype(BF16)
        k_ref[0, h] = (kk[:, sl] + k_rope).astype(BF16)
        vt_ref[0, h] = (vvt[sl, :] + ones_rows).astype(BF16)


def _proj_call(x, p, tm):
    b, s, _ = x.shape
    full = lambda shape: pl.BlockSpec(shape, lambda i, j: (0,) * len(shape))
    hs = pl.BlockSpec((1, N_HEADS, tm, HEAD_PAD), lambda i, j: (i, 0, j, 0))
    return pl.pallas_call(
        _proj_kernel,
        grid=(b, s // tm),
        in_specs=[
            pl.BlockSpec((1, tm, D_MODEL), lambda i, j: (i, j, 0)),
            full((1, D_MODEL)), full((1, D_MODEL)),
            full(p["w_in"].shape),
            full((1, Q_LORA_RANK)), full(p["w_uq"].shape),
            full((1, KV_LORA_RANK)), full(p["w_k"].shape), full(p["w_vt"].shape),
            pl.BlockSpec((tm, HEAD_PAD), lambda i, j: (j, 0)),
            pl.BlockSpec((tm, HEAD_PAD), lambda i, j: (j, 0)),
        ],
        out_specs=[
            pl.BlockSpec((1, tm, POOL_DIM), lambda i, j: (i, j, 0)),
            hs, hs, pl.BlockSpec((1, N_HEADS, HEAD_PAD, tm), lambda i, j: (i, 0, 0, j)),
        ],
        out_shape=[
            jax.ShapeDtypeStruct((b, s, POOL_DIM), F32),
            jax.ShapeDtypeStruct((b, N_HEADS, s, HEAD_PAD), BF16),
            jax.ShapeDtypeStruct((b, N_HEADS, s, HEAD_PAD), BF16),
            jax.ShapeDtypeStruct((b, N_HEADS, HEAD_PAD, s), BF16),
        ],
        compiler_params=pltpu.CompilerParams(
            dimension_semantics=("arbitrary", "arbitrary"), vmem_limit_bytes=VMEM_LIMIT),
        name="proj",
    )(x, p["ln_in_g"], p["ln_in_b"], p["w_in"], p["q_norm_g"], p["w_uq"], p["kv_norm_g"],
      p["w_k"], p["w_vt"], p["cos"][s], p["sin"][s])


def _attn_kernel(q_ref, k_ref, vt_ref, o_ref, m_ref, acc_ref, s_ref, p_ref, alpha_ref, cmax_ref, *, tk):
    s_len = k_ref.shape[2]
    tq = q_ref.shape[2]
    n_chunks = s_len // tk
    m_ref[...] = jnp.full(m_ref.shape, -jnp.inf, F32)
    acc_ref[...] = jnp.zeros(acc_ref.shape, F32)

    def scores(j, slot):
        start = j * tk
        for hh in range(HEADS_PER_STEP):
            k = k_ref[0, hh, pl.ds(start, tk), :]
            st = lax.dot_general(k, q_ref[0, hh], (((1,), (1,)), ((), ())),
                                 preferred_element_type=F32)
            s_ref[slot, hh] = st
            cmax_ref[slot, hh] = jnp.max(st, axis=0, keepdims=True)

    def softmax(slot):
        for hh in range(HEADS_PER_STEP):
            m_old = m_ref[hh]
            m_new = jnp.maximum(m_old, cmax_ref[slot, hh])
            p_ref[slot, hh] = jnp.exp2(s_ref[slot, hh] - m_new).astype(BF16)
            alpha_ref[slot, hh] = jnp.exp2(m_old - m_new)
            m_ref[hh] = m_new

    def values(j, slot):
        start = j * tk
        for hh in range(HEADS_PER_STEP):
            vt = vt_ref[0, hh, :, pl.ds(start, tk)]
            acc_ref[hh] = acc_ref[hh] * alpha_ref[slot, hh] + jnp.dot(
                vt, p_ref[slot, hh], preferred_element_type=F32)

    scores(0, 0)
    for j in range(n_chunks):
        if j + 1 < n_chunks:
            scores(j + 1, (j + 1) % 2)
        if j >= 1:
            values(j - 1, (j - 1) % 2)
        softmax(j % 2)
    values(n_chunks - 1, (n_chunks - 1) % 2)
    heads = []
    for hh in range(HEADS_PER_STEP):
        acc = acc_ref[hh]
        heads.append((acc / acc[V_HEAD_DIM:V_HEAD_DIM + 1, :]).T)
    lane = lax.broadcasted_iota(jnp.int32, (tq, HEAD_PAD), 1)
    out = jnp.where(lane < V_HEAD_DIM, heads[0], pltpu.roll(heads[1], V_HEAD_DIM, 1))
    o_ref[0] = out.astype(BF16)


def _attn_call(q, k, vt, tq, tk):
    b, h, s, _ = q.shape
    hp = HEADS_PER_STEP
    return pl.pallas_call(
        functools.partial(_attn_kernel, tk=tk),
        grid=(b, h // hp, s // tq),
        in_specs=[
            pl.BlockSpec((1, hp, tq, HEAD_PAD), lambda i, j, t: (i, j, t, 0)),
            pl.BlockSpec((1, hp, s, HEAD_PAD), lambda i, j, t: (i, j, 0, 0)),
            pl.BlockSpec((1, hp, HEAD_PAD, s), lambda i, j, t: (i, j, 0, 0)),
        ],
        out_specs=pl.BlockSpec((1, tq, hp * V_HEAD_DIM), lambda i, j, t: (i, t, j)),
        out_shape=jax.ShapeDtypeStruct((b, s, ATTN_DIM), BF16),
        scratch_shapes=[
            pltpu.VMEM((hp, 1, tq), F32),
            pltpu.VMEM((hp, HEAD_PAD, tq), F32),
            pltpu.VMEM((2, hp, tk, tq), F32),
            pltpu.VMEM((2, hp, tk, tq), BF16),
            pltpu.VMEM((2, hp, 1, tq), F32),
            pltpu.VMEM((2, hp, 1, tq), F32),
        ],
        compiler_params=pltpu.CompilerParams(
            dimension_semantics=("arbitrary", "arbitrary", "arbitrary"), vmem_limit_bytes=VMEM_LIMIT),
        name="attn",
    )(q, k, vt)


def _mix_kernel(x_ref, up_ref, um_ref, un_ref, at_ref, lng_ref, lnb_ref, wp_ref, ps_ref, wo_ref,
                g1_ref, b1_ref, wr_ref, x1_ref, x1b_ref, aff_ref, ubuf):
    tm = um_ref.shape[1]
    j = pl.program_id(1)
    n_tiles = pl.num_programs(1)
    s_len = n_tiles * tm
    xn = _layer_norm(x_ref[0], lng_ref[...], lnb_ref[...])

    ubuf[pl.ds(0, HALO), :] = jnp.where(j > 0, up_ref[0], 0.0)
    ubuf[pl.ds(HALO, tm), :] = um_ref[0]
    ubuf[pl.ds(HALO + tm, HALO), :] = jnp.where(j < n_tiles - 1, un_ref[0], 0.0)

    pos = j * tm + lax.broadcasted_iota(jnp.int32, (tm, 1), 0)
    mix = jnp.dot(at_ref[0], wo_ref[pl.ds(POOL_DIM, ATTN_DIM), :], preferred_element_type=F32)
    for g, w in enumerate(POOL_WINDOWS):
        cols = pl.ds(g * POOL_GROUP_DIM, POOL_GROUP_DIM)
        win_sum = ubuf[pl.ds(HALO - w // 2, tm), cols]
        for off in range(-w // 2 + 1, w // 2):
            win_sum = win_sum + ubuf[pl.ds(HALO + off, tm), cols]
        lo = jnp.maximum(pos - w // 2, 0)
        hi = jnp.minimum(pos + w // 2 - 1, s_len - 1)
        cnt = (hi - lo + 1).astype(F32)
        d = win_sum / cnt - ubuf[pl.ds(HALO, tm), cols]
        y = jnp.dot(d.astype(BF16), wp_ref[g], preferred_element_type=F32) * ps_ref[:, cols]
        mix = mix + jnp.dot(y.astype(BF16), wo_ref[cols, :], preferred_element_type=F32)

    x1 = _layer_norm(DEEPNORM_ALPHA * xn + mix, g1_ref[...], b1_ref[...])
    x1_ref[0] = x1
    x1_hi = x1.astype(BF16)
    x1b_ref[0] = x1_hi
    x1_lo = (x1 - x1_hi.astype(F32)).astype(BF16)
    logits = (jnp.dot(x1_hi, wr_ref[0], preferred_element_type=F32)
              + jnp.dot(x1_hi, wr_ref[1], preferred_element_type=F32)
              + jnp.dot(x1_lo, wr_ref[0], preferred_element_type=F32))
    e = jnp.exp(logits - jnp.max(logits, axis=-1, keepdims=True))
    aff_ref[0] = e / jnp.sum(e, axis=-1, keepdims=True)


def _mix_call(x, u, attn, p, tm):
    b, s, _ = x.shape
    full = lambda shape: pl.BlockSpec(shape, lambda i, j: (0,) * len(shape))
    tile = lambda width: pl.BlockSpec((1, tm, width), lambda i, j: (i, j, 0))
    hb = tm // HALO
    last_halo_block = s // HALO - 1
    return pl.pallas_call(
        _mix_kernel,
        grid=(b, s // tm),
        in_specs=[
            tile(D_MODEL),
            pl.BlockSpec((1, HALO, POOL_DIM), lambda i, j: (i, jnp.maximum(j * hb - 1, 0), 0)),
            tile(POOL_DIM),
            pl.BlockSpec((1, HALO, POOL_DIM), lambda i, j: (i, jnp.minimum((j + 1) * hb, last_halo_block), 0)),
            tile(ATTN_DIM),
            full((1, D_MODEL)), full((1, D_MODEL)),
            full(p["w_pool"].shape), full((1, POOL_DIM)), full(p["w_o"].shape),
            full((1, D_MODEL)), full((1, D_MODEL)), full(p["w_router"].shape),
        ],
        out_specs=[tile(D_MODEL), tile(D_MODEL), tile(N_EXPERTS)],
        out_shape=[
            jax.ShapeDtypeStruct((b, s, D_MODEL), F32),
            jax.ShapeDtypeStruct((b, s, D_MODEL), BF16),
            jax.ShapeDtypeStruct((b, s, N_EXPERTS), F32),
        ],
        scratch_shapes=[pltpu.VMEM((tm + 2 * HALO, POOL_DIM), F32)],
        compiler_params=pltpu.CompilerParams(
            dimension_semantics=("arbitrary", "arbitrary"), vmem_limit_bytes=VMEM_LIMIT),
        name="mix",
    )(x, u, u, u, attn, p["ln_in_g"], p["ln_in_b"], p["w_pool"], p["pool_scale"], p["w_o"],
      p["ln1_g"], p["ln1_b"], p["w_router"])


def _select_kernel(aff_ref, gate_ref, rows_ref, offs_ref, *, cap, chunk):
    n_exp, t = aff_ref.shape
    n_chunks = t // chunk

    def count(pred):
        def body(c, acc):
            start = pl.multiple_of(c * chunk, chunk)
            bits = lax.bitcast_convert_type(aff_ref[:, pl.ds(start, chunk)], jnp.int32)
            tok = start + lax.broadcasted_iota(jnp.int32, (n_exp, chunk), 1)
            hits = jnp.where(pred(bits, tok), 1.0, 0.0)
            for k in range(chunk // LANES):
                acc = acc + hits[:, k * LANES:(k + 1) * LANES]
            return acc
        partial = lax.fori_loop(0, n_chunks, body, jnp.zeros((n_exp, LANES), F32))
        return jnp.sum(partial, axis=1, keepdims=True)

    def value_step(i, thr):
        cand = thr | (1 << (30 - i))
        n_ge = count(lambda bits, tok: bits >= cand)
        return jnp.where(n_ge >= cap, cand, thr)
    thr = lax.fori_loop(0, 31, value_step, jnp.zeros((n_exp, 1), jnp.int32))

    need = cap - count(lambda bits, tok: bits > thr)
    index_bits = t.bit_length()

    def index_step(i, below):
        cand = below + (1 << (index_bits - 1 - i))
        n_ties = count(lambda bits, tok: (bits == thr) & (tok < cand))
        return jnp.where((cand <= t) & (n_ties < need), cand, below)
    cut = lax.fori_loop(0, index_bits, index_step, jnp.zeros((n_exp, 1), jnp.int32)) + 1

    def write(c, counts):
        start = pl.multiple_of(c * chunk, chunk)
        aff = aff_ref[:, pl.ds(start, chunk)]
        bits = lax.bitcast_convert_type(aff, jnp.int32)
        tok = start + lax.broadcasted_iota(jnp.int32, (n_exp, chunk), 1)
        sel = (bits > thr) | ((bits == thr) & (tok < cut))
        gate = jnp.where(sel, aff, 0.0)
        gate_ref[:, pl.ds(start, chunk)] = gate
        tile_lane = lax.broadcasted_iota(jnp.int32, counts.shape, 1)
        for k in range(chunk // MOE_TILE):
            n = jnp.sum(jnp.where(gate[:, k * MOE_TILE:(k + 1) * MOE_TILE] > 0.0, 1.0, 0.0),
                        axis=1, keepdims=True)
            counts = jnp.where(tile_lane == c * (chunk // MOE_TILE) + k, n, counts)
        return counts
    n_tiles = t // MOE_TILE
    counts = lax.fori_loop(0, n_chunks, write, jnp.zeros((n_exp, n_tiles), F32))

    rows = jnp.floor((counts + (ROW_ALIGN - 1)) * (1.0 / ROW_ALIGN)) * ROW_ALIGN
    before = (lax.broadcasted_iota(jnp.int32, (n_tiles, n_tiles), 0)
              < lax.broadcasted_iota(jnp.int32, (n_tiles, n_tiles), 1))
    offs = jnp.dot(rows.astype(BF16), jnp.where(before, 1.0, 0.0).astype(BF16), preferred_element_type=F32)
    rows_ref[...] = rows.astype(jnp.int32)
    offs_ref[...] = offs.astype(jnp.int32)


def _select_call(aff_t, cap):
    n_exp, t = aff_t.shape
    n_tiles = t // MOE_TILE
    return pl.pallas_call(
        functools.partial(_select_kernel, cap=cap, chunk=2048),
        out_shape=[
            jax.ShapeDtypeStruct((n_exp, t), F32),
            jax.ShapeDtypeStruct((n_exp, n_tiles), jnp.int32),
            jax.ShapeDtypeStruct((n_exp, n_tiles), jnp.int32),
        ],
        compiler_params=pltpu.CompilerParams(vmem_limit_bytes=VMEM_LIMIT),
        name="select",
    )(aff_t)


def _pack_bf16_pair(x):
    half = x.shape[1] // 2
    hi = lax.bitcast_convert_type(x[:, :half], jnp.uint32) & jnp.uint32(0xFFFF0000)
    lo = lax.shift_right_logical(lax.bitcast_convert_type(x[:, half:], jnp.uint32), jnp.uint32(16))
    return hi | lo


def _unpack_bf16_pair(u):
    a = lax.bitcast_convert_type(u & jnp.uint32(0xFFFF0000), F32)
    b = lax.bitcast_convert_type(lax.shift_left(u, jnp.uint32(16)), F32)
    return a.astype(BF16), b.astype(BF16)


def _scalar_max(ref, i):
    m = ref[0, i]
    for e in range(1, N_EXPERTS):
        m = jnp.maximum(m, ref[e, i])
    return m


def _window_copies(ref_of, rows_ref, offs_ref, i, p, buf, sem, to_hbm, only_if_needed):
    copies = []
    for e in range(N_EXPERTS):
        start = pl.multiple_of(offs_ref[e, i] + p * MOE_WIN, ROW_ALIGN)
        hbm = ref_of.at[e, pl.ds(start, MOE_WIN)]
        vmem = buf.at[pl.ds(e * MOE_WIN, MOE_WIN)]
        copy = pltpu.make_async_copy(vmem, hbm, sem) if to_hbm else pltpu.make_async_copy(hbm, vmem, sem)
        needed = rows_ref[e, i] > p * MOE_WIN
        copies.append((copy, needed if only_if_needed else None))
    return copies


def _start_all(copies):
    for copy, cond in copies:
        if cond is None:
            copy.start()
        else:
            pl.when(cond)(copy.start)


def _wait_all(copies):
    for copy, cond in copies:
        if cond is None:
            copy.wait()
        else:
            pl.when(cond)(copy.wait)


def _dispatch_kernel(rows_ref, offs_ref, gt_ref, xb_ref, xe_ref, stack_ref, zero_ref, sem, zero_sem):
    i = pl.program_id(0)
    n_tiles = pl.num_programs(0)
    slot = i % 2
    n_rows = N_EXPERTS * MOE_WIN
    sel_t = jnp.where(gt_ref[...] > 0.0, 1.0, 0.0).astype(BF16)
    earlier = (lax.broadcasted_iota(jnp.int32, (MOE_TILE, MOE_TILE), 0)
               < lax.broadcasted_iota(jnp.int32, (MOE_TILE, MOE_TILE), 1))
    rank_t = jnp.dot(sel_t, jnp.where(earlier, 1.0, 0.0).astype(BF16), preferred_element_type=F32)
    spread = jnp.where(lax.broadcasted_iota(jnp.int32, (n_rows, N_EXPERTS), 0) // MOE_WIN
                       == lax.broadcasted_iota(jnp.int32, (n_rows, N_EXPERTS), 1), 1.0, 0.0).astype(BF16)
    rank_rows = jnp.dot(spread, rank_t.astype(BF16), preferred_element_type=F32)
    sel_rows = jnp.dot(spread, sel_t, preferred_element_type=F32)
    slot_in_win = lax.broadcasted_iota(jnp.int32, (n_rows, 1), 0) % MOE_WIN

    def fill(p):
        want = (slot_in_win + p * MOE_WIN).astype(F32)
        onehot = jnp.where((rank_rows == want) & (sel_rows > 0.0), 1.0, 0.0).astype(BF16)
        picked = jnp.dot(onehot, xb_ref[...], preferred_element_type=F32)
        stack_ref[slot] = _pack_bf16_pair(picked)

    copies = lambda tile, p, s, only_if_needed: _window_copies(
        xe_ref, rows_ref, offs_ref, tile, p, stack_ref.at[s], sem.at[s], True, only_if_needed)
    passes = lambda tile: (_scalar_max(rows_ref, tile) + MOE_WIN - 1) // MOE_WIN

    fill(0)
    prev = jnp.maximum(i - 1, 0)

    @pl.when((i > 0) & (passes(prev) <= 1))
    def _():
        _wait_all(copies(prev, 0, 1 - slot, False))

    _start_all(copies(i, 0, slot, False))
    n_pass = passes(i)

    @pl.when(n_pass > 1)
    def _():
        _wait_all(copies(i, 0, slot, False))

        def extra(p, carry):
            fill(p)
            _start_all(copies(i, p, slot, True))
            _wait_all(copies(i, p, slot, True))
            return carry
        lax.fori_loop(1, n_pass, extra, 0)

    @pl.when((i == n_tiles - 1) & (n_pass <= 1))
    def _():
        _wait_all(copies(i, 0, slot, False))

    @pl.when(i == n_tiles - 1)
    def _():
        cap_rows = xe_ref.shape[1]
        zero_ref[...] = jnp.zeros(zero_ref.shape, jnp.uint32)
        for e in range(N_EXPERTS):
            total = offs_ref[e, i] + rows_ref[e, i]
            n_full = (cap_rows - total) // MOE_WIN

            def zero_copy(start, e=e):
                return pltpu.make_async_copy(
                    zero_ref, xe_ref.at[e, pl.ds(pl.multiple_of(start, ROW_ALIGN), MOE_WIN)], zero_sem)

            def fill_start(k, carry, total=total, zero_copy=zero_copy):
                zero_copy(total + k * MOE_WIN).start()
                return carry

            def fill_wait(k, carry, zero_copy=zero_copy):
                zero_copy(0).wait()
                return carry

            lax.fori_loop(0, n_full, fill_start, 0)
            lax.fori_loop(0, n_full, fill_wait, 0)
            zero_copy(cap_rows - MOE_WIN).start()
            zero_copy(cap_rows - MOE_WIN).wait()


def _dispatch_call(rows, offs, gates_t, x1b, cap_rows):
    n_exp, t = gates_t.shape
    n_tiles = t // MOE_TILE
    grid_spec = pltpu.PrefetchScalarGridSpec(
        num_scalar_prefetch=2,
        grid=(n_tiles,),
        in_specs=[
            pl.BlockSpec((n_exp, MOE_TILE), lambda i, rows, offs: (0, i)),
            pl.BlockSpec((MOE_TILE, D_MODEL), lambda i, rows, offs: (i, 0)),
        ],
        out_specs=pl.BlockSpec(memory_space=pl.ANY),
        scratch_shapes=[
            pltpu.VMEM((2, n_exp * MOE_WIN, D_MODEL // 2), jnp.uint32),
            pltpu.VMEM((MOE_WIN, D_MODEL // 2), jnp.uint32),
            pltpu.SemaphoreType.DMA((2,)),
            pltpu.SemaphoreType.DMA(()),
        ],
    )
    return pl.pallas_call(
        _dispatch_kernel,
        grid_spec=grid_spec,
        out_shape=jax.ShapeDtypeStruct((n_exp, cap_rows, D_MODEL // 2), jnp.uint32),
        compiler_params=pltpu.CompilerParams(
            dimension_semantics=("arbitrary",), vmem_limit_bytes=VMEM_LIMIT),
        name="dispatch",
    )(rows, offs, gates_t, x1b)


def _ffn_kernel(total_ref, xe_ref, wg_ref, wu_ref, wd_ref, ye_ref):
    e = pl.program_id(0)
    blk = pl.program_id(1)
    n_blk_rows = xe_ref.shape[1]
    total = total_ref[e]

    @pl.when(blk * n_blk_rows < total)
    def _():
        xa, xb = _unpack_bf16_pair(xe_ref[0])
        half = D_MODEL // 2
        hg = (jnp.dot(xa, wg_ref[0, :half, :], preferred_element_type=F32)
              + jnp.dot(xb, wg_ref[0, half:, :], preferred_element_type=F32))
        hu = (jnp.dot(xa, wu_ref[0, :half, :], preferred_element_type=F32)
              + jnp.dot(xb, wu_ref[0, half:, :], preferred_element_type=F32))
        h = hg * jax.nn.sigmoid(hg) * hu
        ye = jnp.dot(h.astype(BF16), wd_ref[0], preferred_element_type=F32)
        ye_ref[0] = _pack_bf16_pair(ye.astype(BF16).astype(F32))

    @pl.when(blk * n_blk_rows >= total)
    def _():
        ye_ref[...] = jnp.zeros(ye_ref.shape, jnp.uint32)


def _ffn_call(totals, xe, p):
    n_exp, cap_rows, half = xe.shape
    n_blk = cap_rows // FFN_ROWS

    def xe_index(e, b, totals):
        last = jnp.maximum((totals[e] + FFN_ROWS - 1) // FFN_ROWS - 1, 0)
        return (e, jnp.minimum(b, last), 0)

    grid_spec = pltpu.PrefetchScalarGridSpec(
        num_scalar_prefetch=1,
        grid=(n_exp, n_blk),
        in_specs=[
            pl.BlockSpec((1, FFN_ROWS, half), xe_index),
            pl.BlockSpec((1, D_MODEL, FFN_HIDDEN), lambda e, b, totals: (e, 0, 0)),
            pl.BlockSpec((1, D_MODEL, FFN_HIDDEN), lambda e, b, totals: (e, 0, 0)),
            pl.BlockSpec((1, FFN_HIDDEN, D_MODEL), lambda e, b, totals: (e, 0, 0)),
        ],
        out_specs=pl.BlockSpec((1, FFN_ROWS, half), lambda e, b, totals: (e, b, 0)),
    )
    return pl.pallas_call(
        _ffn_kernel,
        grid_spec=grid_spec,
        out_shape=jax.ShapeDtypeStruct(xe.shape, jnp.uint32),
        compiler_params=pltpu.CompilerParams(
            dimension_semantics=("arbitrary", "arbitrary"), vmem_limit_bytes=VMEM_LIMIT),
        name="ffn",
    )(totals, xe, p["w_gate"], p["w_up"], p["w_down"])


def _combine_kernel(rows_ref, offs_ref, g_ref, x1_ref, g2_ref, b2_ref, ye_ref, o_ref, land_ref, moe_ref, sem):
    i = pl.program_id(0)
    n_tiles = pl.num_programs(0)
    slot = i % 2
    n_rows = N_EXPERTS * MOE_WIN
    half = D_MODEL // 2
    fetch = lambda tile, p, s, needed: _window_copies(ye_ref, rows_ref, offs_ref, tile, p, land_ref.at[s],
                                                      sem.at[s], False, needed)

    @pl.when(i == 0)
    def _():
        _start_all(fetch(0, 0, 0, False))

    @pl.when(i + 1 < n_tiles)
    def _():
        _start_all(fetch(jnp.minimum(i + 1, n_tiles - 1), 0, 1 - slot, False))

    gates = g_ref[...]
    sel = jnp.where(gates > 0.0, 1.0, 0.0).astype(BF16)
    earlier = (lax.broadcasted_iota(jnp.int32, (MOE_TILE, MOE_TILE), 1)
               < lax.broadcasted_iota(jnp.int32, (MOE_TILE, MOE_TILE), 0))
    rank = jnp.dot(jnp.where(earlier, 1.0, 0.0).astype(BF16), sel, preferred_element_type=F32)
    spread = jnp.where(lax.broadcasted_iota(jnp.int32, (N_EXPERTS, n_rows), 1) // MOE_WIN
                       == lax.broadcasted_iota(jnp.int32, (N_EXPERTS, n_rows), 0), 1.0, 0.0).astype(BF16)
    rank_cols = jnp.dot(rank.astype(BF16), spread, preferred_element_type=F32)
    gate_cols = jnp.dot(gates.astype(BF16), spread, preferred_element_type=F32)
    slot_in_win = lax.broadcasted_iota(jnp.int32, (1, n_rows), 1) % MOE_WIN
    row_expert = lax.broadcasted_iota(jnp.int32, (n_rows, 1), 0) // MOE_WIN
    row_in_win = lax.broadcasted_iota(jnp.int32, (n_rows, 1), 0) % MOE_WIN
    seg_rows = jnp.zeros((n_rows, 1), jnp.int32)
    for e in range(N_EXPERTS):
        seg_rows = jnp.where(row_expert == e, rows_ref[e, i], seg_rows)

    def gather(p, s):
        want = (slot_in_win + p * MOE_WIN).astype(F32)
        weights = jnp.where(rank_cols == want, gate_cols, 0.0).astype(BF16)
        landed = jnp.where(row_in_win + p * MOE_WIN < seg_rows, land_ref[s], jnp.uint32(0))
        ya, yb = _unpack_bf16_pair(landed)
        return (jnp.dot(weights, ya, preferred_element_type=F32),
                jnp.dot(weights, yb, preferred_element_type=F32))

    _wait_all(fetch(i, 0, slot, False))
    ma, mb = gather(0, slot)
    moe_ref[:, :half] = ma
    moe_ref[:, half:] = mb
    n_pass = (_scalar_max(rows_ref, i) + MOE_WIN - 1) // MOE_WIN

    @pl.when(n_pass > 1)
    def _():
        def extra(p, carry):
            _start_all(fetch(i, p, 2, True))
            _wait_all(fetch(i, p, 2, True))
            ea, eb = gather(p, 2)
            moe_ref[:, :half] += ea
            moe_ref[:, half:] += eb
            return carry
        lax.fori_loop(1, n_pass, extra, 0)

    o_ref[...] = _layer_norm(DEEPNORM_ALPHA * x1_ref[...] + moe_ref[...], g2_ref[...], b2_ref[...])


def _combine_call(rows, offs, gates, x1, ye, p):
    t, n_exp = gates.shape
    n_tiles = t // MOE_TILE
    full = lambda shape: pl.BlockSpec(shape, lambda i, rows, offs: (0,) * len(shape))
    grid_spec = pltpu.PrefetchScalarGridSpec(
        num_scalar_prefetch=2,
        grid=(n_tiles,),
        in_specs=[
            pl.BlockSpec((MOE_TILE, n_exp), lambda i, rows, offs: (i, 0)),
            pl.BlockSpec((MOE_TILE, D_MODEL), lambda i, rows, offs: (i, 0)),
            full((1, D_MODEL)), full((1, D_MODEL)),
            pl.BlockSpec(memory_space=pl.ANY),
        ],
        out_specs=pl.BlockSpec((MOE_TILE, D_MODEL), lambda i, rows, offs: (i, 0)),
        scratch_shapes=[
            pltpu.VMEM((3, n_exp * MOE_WIN, D_MODEL // 2), jnp.uint32),
            pltpu.VMEM((MOE_TILE, D_MODEL), F32),
            pltpu.SemaphoreType.DMA((3,)),
        ],
    )
    return pl.pallas_call(
        _combine_kernel,
        grid_spec=grid_spec,
        out_shape=jax.ShapeDtypeStruct((t, D_MODEL), F32),
        compiler_params=pltpu.CompilerParams(
            dimension_semantics=("arbitrary",), vmem_limit_bytes=VMEM_LIMIT),
        name="combine",
    )(rows, offs, gates, x1, p["ln2_g"], p["ln2_b"], ye)


def _rope_tables(s):
    half = QK_ROPE_DIM // 2
    inv = 1.0 / (ROPE_THETA ** (jnp.arange(0, QK_ROPE_DIM, 2, dtype=F32) / QK_ROPE_DIM))
    ang = jnp.arange(s, dtype=F32)[:, None] * inv[None, :]
    cos, sin = jnp.cos(ang), jnp.sin(ang)
    tail = HEAD_PAD - QK_NOPE_DIM - QK_ROPE_DIM
    cos_t = jnp.concatenate([jnp.ones((s, QK_NOPE_DIM), F32), cos, cos, jnp.ones((s, tail), F32)], axis=1)
    sin_t = jnp.concatenate([jnp.zeros((s, QK_NOPE_DIM), F32), -sin, sin, jnp.zeros((s, tail), F32)], axis=1)
    assert half * 2 == QK_ROPE_DIM
    return cos_t, sin_t


def _prepare_params(seq_lens, ln_in_g, ln_in_b, w_in, w_pool, pool_scale, q_norm_g, w_uq, kv_norm_g, w_ukv,
                    w_o, ln1_g, ln1_b, w_router, w_gate, w_up, w_down, ln2_g, ln2_b):
    row = lambda a: a.reshape(1, -1).astype(F32)
    o3 = POOL_DIM + Q_LORA_RANK + KV_LORA_RANK
    w_in0 = w_in[0]
    w_in_p = jnp.concatenate([
        w_in0[:, :o3], jnp.zeros((D_MODEL, QK_NOPE_DIM), F32), w_in0[:, o3:],
        jnp.zeros((D_MODEL, HEAD_PAD - QK_NOPE_DIM - QK_ROPE_DIM), F32)], axis=1)
    qk = QK_NOPE_DIM + QK_ROPE_DIM
    w_uq_p = jnp.pad(w_uq[0].reshape(Q_LORA_RANK, N_HEADS, qk), ((0, 0), (0, 0), (0, HEAD_PAD - qk)))
    w_ukv0 = w_ukv[0].reshape(KV_LORA_RANK, N_HEADS, QK_NOPE_DIM + V_HEAD_DIM)
    w_k = jnp.pad(w_ukv0[:, :, :QK_NOPE_DIM], ((0, 0), (0, 0), (0, HEAD_PAD - QK_NOPE_DIM)))
    w_v = jnp.pad(w_ukv0[:, :, QK_NOPE_DIM:], ((0, 0), (0, 0), (0, HEAD_PAD - V_HEAD_DIM)))
    tables = {s: _rope_tables(s) for s in seq_lens}
    return {
        "ln_in_g": row(ln_in_g), "ln_in_b": row(ln_in_b),
        "w_in": w_in_p.astype(BF16),
        "w_pool": w_pool[0].astype(BF16), "pool_scale": row(pool_scale[0]),
        "q_norm_g": row(q_norm_g[0]), "w_uq": w_uq_p.reshape(Q_LORA_RANK, N_HEADS * HEAD_PAD).astype(BF16),
        "kv_norm_g": row(kv_norm_g[0]),
        "w_k": w_k.reshape(KV_LORA_RANK, N_HEADS * HEAD_PAD).astype(BF16),
        "w_vt": w_v.reshape(KV_LORA_RANK, N_HEADS * HEAD_PAD).T.astype(BF16),
        "w_o": w_o[0].astype(BF16),
        "ln1_g": row(ln1_g[0]), "ln1_b": row(ln1_b[0]),
        "w_router": jnp.stack([w_router[0].astype(BF16),
                               (w_router[0] - w_router[0].astype(BF16).astype(F32)).astype(BF16)]),
        "w_gate": w_gate[0].astype(BF16), "w_up": w_up[0].astype(BF16), "w_down": w_down[0].astype(BF16),
        "ln2_g": row(ln2_g[0]), "ln2_b": row(ln2_b[0]),
        "cos": {s: t[0] for s, t in tables.items()},
        "sin": {s: t[1] for s, t in tables.items()},
    }


def _trunk(x, p):
    b, s, d = x.shape
    t = b * s
    u, q, k, vt = _proj_call(x, p, tm=1024)
    attn = _attn_call(q, k, vt, tq=512, tk=1024)
    x1, x1b, aff = _mix_call(x, u, attn, p, tm=1024)
    cap = max(1, (CAPACITY_FACTOR * t) // N_EXPERTS)
    gates_t, rows, offs = _select_call(aff.reshape(t, N_EXPERTS).T, cap)
    cap_rows = -(-(cap + ROW_ALIGN * (t // MOE_TILE) + MOE_WIN) // FFN_ROWS) * FFN_ROWS
    xe = _dispatch_call(rows, offs, gates_t, x1b.reshape(t, d), cap_rows)
    totals = offs[:, -1] + rows[:, -1]
    ye = _ffn_call(totals, xe, p)
    y = _combine_call(rows, offs, gates_t.T, x1.reshape(t, d), ye, p)
    return y.reshape(b, s, d)


def kernel(x_prompt, x_sample, ln_in_g, ln_in_b, w_in, w_pool, pool_scale, q_norm_g, w_uq, kv_norm_g, w_ukv,
           w_o, ln1_g, ln1_b, w_router, w_gate, w_up, w_down, ln2_g, ln2_b):
    assert w_in.shape[0] == DEPTH == 1
    p = _prepare_params((x_prompt.shape[1], x_sample.shape[1]), ln_in_g, ln_in_b, w_in, w_pool, pool_scale,
                        q_norm_g, w_uq, kv_norm_g, w_ukv, w_o, ln1_g, ln1_b, w_router, w_gate, w_up, w_down,
                        ln2_g, ln2_b)
    return (_trunk(x_prompt, p), _trunk(x_sample, p))
```

```python
import functools
import math

import jax
import jax.numpy as jnp
from jax import lax
from jax.experimental import pallas as pl
from jax.experimental.pallas import tpu as pltpu

D_MODEL = 1024
POOL_WINDOWS = (2, 4, 8, 16)
POOL_DIM = 512
POOL_GROUP_DIM = 128
N_HEADS = 8
QK_NOPE_DIM = 64
QK_ROPE_DIM = 32
V_HEAD_DIM = 64
ATTN_DIM = N_HEADS * V_HEAD_DIM
Q_LORA_RANK = 384
KV_LORA_RANK = 256
ROPE_THETA = 10000.0
N_EXPERTS = 16
CAPACITY_FACTOR = 2
FFN_HIDDEN = 512
LN_EPS = 1e-5
RMS_EPS = 1e-6
DEPTH = 1
DEEPNORM_ALPHA = (2.0 * DEPTH) ** 0.25

LANES = 128
HEAD_PAD = 128
V_ROWS = 80
HEADS_PER_STEP = 2
MOE_TILE = 256
MOE_WIN = 64
ROW_ALIGN = 8
FFN_ROWS = 512
HALO = 8
VMEM_LIMIT = 56 * 1024 * 1024

BF16 = jnp.bfloat16
F32 = jnp.float32


def _layer_norm(x, g, b):
    mu = jnp.mean(x, axis=-1, keepdims=True)
    xc = x - mu
    var = jnp.mean(xc * xc, axis=-1, keepdims=True)
    return xc * lax.rsqrt(var + LN_EPS) * g + b


def _rms_norm(x, g):
    return x * lax.rsqrt(jnp.mean(x * x, axis=-1, keepdims=True) + RMS_EPS) * g


def _proj_kernel(x_ref, lng_ref, lnb_ref, win_ref, qg_ref, wuq_ref, kvg_ref, wk_ref, wvt_ref,
                 cos_ref, sin_ref, u_ref, q_ref, k_ref, vt_ref):
    xn = _layer_norm(x_ref[0], lng_ref[...], lnb_ref[...])
    proj = jnp.dot(xn.astype(BF16), win_ref[...], preferred_element_type=F32)
    o1 = POOL_DIM
    o2 = o1 + Q_LORA_RANK
    o3 = o2 + KV_LORA_RANK
    u_ref[0] = proj[:, :o1]
    qn = _rms_norm(proj[:, o1:o2], qg_ref[...])
    q = jnp.dot(qn.astype(BF16), wuq_ref[...], preferred_element_type=F32)
    kvn = _rms_norm(proj[:, o2:o3], kvg_ref[...]).astype(BF16)
    kk = jnp.dot(kvn, wk_ref[...], preferred_element_type=F32)
    vvt = lax.dot_general(wvt_ref[...], kvn, (((1,), (1,)), ((), ())),
                          preferred_element_type=F32)
    cos = cos_ref[...]
    sin = sin_ref[...]
    lane = lax.broadcasted_iota(jnp.int32, cos.shape, 1)
    first_half = lane < QK_NOPE_DIM + QK_ROPE_DIM // 2

    def rope(t):
        partner = jnp.where(first_half,
                            pltpu.roll(t, HEAD_PAD - QK_ROPE_DIM // 2, 1),
                            pltpu.roll(t, QK_ROPE_DIM // 2, 1))
        return t * cos + partner * sin

    k_rope = rope(proj[:, o3:o3 + HEAD_PAD])
    row = lax.broadcasted_iota(jnp.int32, (V_ROWS, vvt.shape[1]), 0)
    ones_rows = jnp.where(row >= V_HEAD_DIM, 1.0, 0.0).astype(F32)
    scale = math.log2(math.e) / math.sqrt(QK_NOPE_DIM + QK_ROPE_DIM)
    for h in range(N_HEADS):
        sl = slice(h * HEAD_PAD, (h + 1) * HEAD_PAD)
        q_ref[0, h] = (rope(q[:, sl]) * scale).astype(BF16)
        k_ref[0, h] = (kk[:, sl] + k_rope).astype(BF16)
        vt_ref[0, h] = (vvt[h * V_ROWS:(h + 1) * V_ROWS, :] + ones_rows).astype(BF16)


def _proj_call(x, p, tm):
    b, s, _ = x.shape
    full = lambda shape: pl.BlockSpec(shape, lambda i, j: (0,) * len(shape))
    hs = pl.BlockSpec((1, N_HEADS, tm, HEAD_PAD), lambda i, j: (i, 0, j, 0))
    return pl.pallas_call(
        _proj_kernel,
        grid=(b, s // tm),
        in_specs=[
            pl.BlockSpec((1, tm, D_MODEL), lambda i, j: (i, j, 0)),
            full((1, D_MODEL)), full((1, D_MODEL)),
            full(p["w_in"].shape),
            full((1, Q_LORA_RANK)), full(p["w_uq"].shape),
            full((1, KV_LORA_RANK)), full(p["w_k"].shape), full(p["w_vt"].shape),
            pl.BlockSpec((tm, HEAD_PAD), lambda i, j: (j, 0)),
            pl.BlockSpec((tm, HEAD_PAD), lambda i, j: (j, 0)),
        ],
        out_specs=[
            pl.BlockSpec((1, tm, POOL_DIM), lambda i, j: (i, j, 0)),
            hs, hs, pl.BlockSpec((1, N_HEADS, V_ROWS, tm), lambda i, j: (i, 0, 0, j)),
        ],
        out_shape=[
            jax.ShapeDtypeStruct((b, s, POOL_DIM), F32),
            jax.ShapeDtypeStruct((b, N_HEADS, s, HEAD_PAD), BF16),
            jax.ShapeDtypeStruct((b, N_HEADS, s, HEAD_PAD), BF16),
            jax.ShapeDtypeStruct((b, N_HEADS, V_ROWS, s), BF16),
        ],
        compiler_params=pltpu.CompilerParams(
            dimension_semantics=("arbitrary", "arbitrary"), vmem_limit_bytes=VMEM_LIMIT),
        name="proj",
    )(x, p["ln_in_g"], p["ln_in_b"], p["w_in"], p["q_norm_g"], p["w_uq"], p["kv_norm_g"],
      p["w_k"], p["w_vt"], p["cos"][s], p["sin"][s])


def _attn_kernel(q_ref, k_ref, vt_ref, o_ref, m_ref, acc_ref, s_ref, p_ref, alpha_ref, cmax_ref, *, tk):
    s_len = k_ref.shape[2]
    tq = q_ref.shape[2]
    n_chunks = s_len // tk
    m_ref[...] = jnp.full(m_ref.shape, -jnp.inf, F32)
    acc_ref[...] = jnp.zeros(acc_ref.shape, F32)

    def scores(j, slot):
        start = j * tk
        for hh in range(HEADS_PER_STEP):
            k = k_ref[0, hh, pl.ds(start, tk), :]
            st = lax.dot_general(k, q_ref[0, hh], (((1,), (1,)), ((), ())),
                                 preferred_element_type=F32)
            s_ref[slot, hh] = st
            cmax_ref[slot, hh] = jnp.max(st, axis=0, keepdims=True)

    def softmax(slot):
        for hh in range(HEADS_PER_STEP):
            m_old = m_ref[hh]
            m_new = jnp.maximum(m_old, cmax_ref[slot, hh])
            p_ref[slot, hh] = jnp.exp2(s_ref[slot, hh] - m_new).astype(BF16)
            alpha_ref[slot, hh] = jnp.exp2(m_old - m_new)
            m_ref[hh] = m_new

    def values(j, slot):
        start = j * tk
        for hh in range(HEADS_PER_STEP):
            vt = vt_ref[0, hh, :, pl.ds(start, tk)]
            acc_ref[hh] = acc_ref[hh] * alpha_ref[slot, hh] + jnp.dot(
                vt, p_ref[slot, hh], preferred_element_type=F32)

    scores(0, 0)
    for j in range(n_chunks):
        if j + 1 < n_chunks:
            scores(j + 1, (j + 1) % 2)
        if j >= 1:
            values(j - 1, (j - 1) % 2)
        softmax(j % 2)
    values(n_chunks - 1, (n_chunks - 1) % 2)
    heads = []
    for hh in range(HEADS_PER_STEP):
        acc = acc_ref[hh]
        normed = acc / acc[V_HEAD_DIM:V_HEAD_DIM + 1, :]
        pad = jnp.zeros((HEAD_PAD - V_ROWS, tq), F32)
        heads.append(jnp.concatenate([normed, pad], axis=0).T)
    lane = lax.broadcasted_iota(jnp.int32, (tq, HEAD_PAD), 1)
    out = jnp.where(lane < V_HEAD_DIM, heads[0], pltpu.roll(heads[1], V_HEAD_DIM, 1))
    o_ref[0] = out.astype(BF16)


def _attn_call(q, k, vt, tq, tk):
    b, h, s, _ = q.shape
    hp = HEADS_PER_STEP
    return pl.pallas_call(
        functools.partial(_attn_kernel, tk=tk),
        grid=(b, h // hp, s // tq),
        in_specs=[
            pl.BlockSpec((1, hp, tq, HEAD_PAD), lambda i, j, t: (i, j, t, 0)),
            pl.BlockSpec((1, hp, s, HEAD_PAD), lambda i, j, t: (i, j, 0, 0)),
            pl.BlockSpec((1, hp, V_ROWS, s), lambda i, j, t: (i, j, 0, 0)),
        ],
        out_specs=pl.BlockSpec((1, tq, hp * V_HEAD_DIM), lambda i, j, t: (i, t, j)),
        out_shape=jax.ShapeDtypeStruct((b, s, ATTN_DIM), BF16),
        scratch_shapes=[
            pltpu.VMEM((hp, 1, tq), F32),
            pltpu.VMEM((hp, V_ROWS, tq), F32),
            pltpu.VMEM((2, hp, tk, tq), F32),
            pltpu.VMEM((2, hp, tk, tq), BF16),
            pltpu.VMEM((2, hp, 1, tq), F32),
            pltpu.VMEM((2, hp, 1, tq), F32),
        ],
        compiler_params=pltpu.CompilerParams(
            dimension_semantics=("arbitrary", "arbitrary", "arbitrary"), vmem_limit_bytes=VMEM_LIMIT),
        name="attn",
    )(q, k, vt)


def _mix_kernel(x_ref, up_ref, um_ref, un_ref, at_ref, lng_ref, lnb_ref, wp_ref, ps_ref, wo_ref,
                g1_ref, b1_ref, wr_ref, x1_ref, x1b_ref, aff_ref, ubuf):
    tm = um_ref.shape[1]
    j = pl.program_id(1)
    n_tiles = pl.num_programs(1)
    s_len = n_tiles * tm
    xn = _layer_norm(x_ref[0], lng_ref[...], lnb_ref[...])

    ubuf[pl.ds(0, HALO), :] = jnp.where(j > 0, up_ref[0], 0.0)
    ubuf[pl.ds(HALO, tm), :] = um_ref[0]
    ubuf[pl.ds(HALO + tm, HALO), :] = jnp.where(j < n_tiles - 1, un_ref[0], 0.0)

    edge = lax.broadcasted_iota(jnp.int32, (HALO, 1), 0)

    def inv_count(pos, w):
        lo = jnp.maximum(pos - w // 2, 0)
        hi = jnp.minimum(pos + w // 2 - 1, s_len - 1)
        return 1.0 / (hi - lo + 1).astype(F32)

    mix = jnp.dot(at_ref[0], wo_ref[pl.ds(POOL_DIM, ATTN_DIM), :], preferred_element_type=F32)
    for g, w in enumerate(POOL_WINDOWS):
        cols = pl.ds(g * POOL_GROUP_DIM, POOL_GROUP_DIM)
        n_buf = tm + 2 * HALO
        acc = ubuf[:, cols]
        span = 1
        while span < w // 2:
            acc = acc + pltpu.roll(acc, n_buf - span, 0)
            span *= 2
        acc = acc + pltpu.roll(acc, w // 2, 0)
        win_sum = acc[HALO:HALO + tm]
        u_g = ubuf[pl.ds(HALO, tm), cols]
        d = jnp.concatenate([
            win_sum[:HALO] * inv_count(j * tm + edge, w) - u_g[:HALO],
            win_sum[HALO:tm - HALO] * (1.0 / w) - u_g[HALO:tm - HALO],
            win_sum[tm - HALO:] * inv_count(j * tm + tm - HALO + edge, w) - u_g[tm - HALO:],
        ], axis=0)
        y = jnp.dot(d.astype(BF16), wp_ref[g], preferred_element_type=F32) * ps_ref[:, cols]
        mix = mix + jnp.dot(y.astype(BF16), wo_ref[cols, :], preferred_element_type=F32)

    x1 = _layer_norm(DEEPNORM_ALPHA * xn + mix, g1_ref[...], b1_ref[...])
    x1_ref[0] = x1
    x1_hi = x1.astype(BF16)
    x1b_ref[0] = x1_hi
    x1_lo = (x1 - x1_hi.astype(F32)).astype(BF16)
    logits = (jnp.dot(x1_hi, wr_ref[0], preferred_element_type=F32)
              + jnp.dot(x1_hi, wr_ref[1], preferred_element_type=F32)
              + jnp.dot(x1_lo, wr_ref[0], preferred_element_type=F32))
    e = jnp.exp(logits - jnp.max(logits, axis=-1, keepdims=True))
    aff_ref[0] = e / jnp.sum(e, axis=-1, keepdims=True)


def _mix_call(x, u, attn, p, tm):
    b, s, _ = x.shape
    full = lambda shape: pl.BlockSpec(shape, lambda i, j: (0,) * len(shape))
    tile = lambda width: pl.BlockSpec((1, tm, width), lambda i, j: (i, j, 0))
    hb = tm // HALO
    last_halo_block = s // HALO - 1
    return pl.pallas_call(
        _mix_kernel,
        grid=(b, s // tm),
        in_specs=[
            tile(D_MODEL),
            pl.BlockSpec((1, HALO, POOL_DIM), lambda i, j: (i, jnp.maximum(j * hb - 1, 0), 0)),
            tile(POOL_DIM),
            pl.BlockSpec((1, HALO, POOL_DIM), lambda i, j: (i, jnp.minimum((j + 1) * hb, last_halo_block), 0)),
            tile(ATTN_DIM),
            full((1, D_MODEL)), full((1, D_MODEL)),
            full(p["w_pool"].shape), full((1, POOL_DIM)), full(p["w_o"].shape),
            full((1, D_MODEL)), full((1, D_MODEL)), full(p["w_router"].shape),
        ],
        out_specs=[tile(D_MODEL), tile(D_MODEL), tile(N_EXPERTS)],
        out_shape=[
            jax.ShapeDtypeStruct((b, s, D_MODEL), F32),
            jax.ShapeDtypeStruct((b, s, D_MODEL), BF16),
            jax.ShapeDtypeStruct((b, s, N_EXPERTS), F32),
        ],
        scratch_shapes=[pltpu.VMEM((tm + 2 * HALO, POOL_DIM), F32)],
        compiler_params=pltpu.CompilerParams(
            dimension_semantics=("arbitrary", "arbitrary"), vmem_limit_bytes=VMEM_LIMIT),
        name="mix",
    )(x, u, u, u, attn, p["ln_in_g"], p["ln_in_b"], p["w_pool"], p["pool_scale"], p["w_o"],
      p["ln1_g"], p["ln1_b"], p["w_router"])


def _select_kernel(aff_ref, gate_ref, rows_ref, offs_ref, *, cap, chunk):
    n_exp, t = aff_ref.shape
    n_chunks = t // chunk

    def count(pred):
        def body(c, acc):
            start = pl.multiple_of(c * chunk, chunk)
            bits = lax.bitcast_convert_type(aff_ref[:, pl.ds(start, chunk)], jnp.int32)
            tok = start + lax.broadcasted_iota(jnp.int32, (n_exp, chunk), 1)
            hits = jnp.where(pred(bits, tok), 1.0, 0.0)
            for k in range(chunk // LANES):
                acc = acc + hits[:, k * LANES:(k + 1) * LANES]
            return acc
        partial = lax.fori_loop(0, n_chunks, body, jnp.zeros((n_exp, LANES), F32))
        return jnp.sum(partial, axis=1, keepdims=True)

    def value_step(i, thr):
        cand = thr | (1 << (30 - i))
        n_ge = count(lambda bits, tok: bits >= cand)
        return jnp.where(n_ge >= cap, cand, thr)
    thr = lax.fori_loop(0, 31, value_step, jnp.zeros((n_exp, 1), jnp.int32))

    need = cap - count(lambda bits, tok: bits > thr)
    index_bits = t.bit_length()

    def index_step(i, below):
        cand = below + (1 << (index_bits - 1 - i))
        n_ties = count(lambda bits, tok: (bits == thr) & (tok < cand))
        return jnp.where((cand <= t) & (n_ties < need), cand, below)
    cut = lax.fori_loop(0, index_bits, index_step, jnp.zeros((n_exp, 1), jnp.int32)) + 1

    def write(c, counts):
        start = pl.multiple_of(c * chunk, chunk)
        aff = aff_ref[:, pl.ds(start, chunk)]
        bits = lax.bitcast_convert_type(aff, jnp.int32)
        tok = start + lax.broadcasted_iota(jnp.int32, (n_exp, chunk), 1)
        sel = (bits > thr) | ((bits == thr) & (tok < cut))
        gate = jnp.where(sel, aff, 0.0)
        gate_ref[:, pl.ds(start, chunk)] = gate
        tile_lane = lax.broadcasted_iota(jnp.int32, counts.shape, 1)
        for k in range(chunk // MOE_TILE):
            n = jnp.sum(jnp.where(gate[:, k * MOE_TILE:(k + 1) * MOE_TILE] > 0.0, 1.0, 0.0),
                        axis=1, keepdims=True)
            counts = jnp.where(tile_lane == c * (chunk // MOE_TILE) + k, n, counts)
        return counts
    n_tiles = t // MOE_TILE
    counts = lax.fori_loop(0, n_chunks, write, jnp.zeros((n_exp, n_tiles), F32))

    rows = jnp.floor((counts + (ROW_ALIGN - 1)) * (1.0 / ROW_ALIGN)) * ROW_ALIGN
    before = (lax.broadcasted_iota(jnp.int32, (n_tiles, n_tiles), 0)
              < lax.broadcasted_iota(jnp.int32, (n_tiles, n_tiles), 1))
    offs = jnp.dot(rows.astype(BF16), jnp.where(before, 1.0, 0.0).astype(BF16), preferred_element_type=F32)
    rows_ref[...] = rows.astype(jnp.int32)
    offs_ref[...] = offs.astype(jnp.int32)


def _select_call(aff_t, cap):
    n_exp, t = aff_t.shape
    n_tiles = t // MOE_TILE
    return pl.pallas_call(
        functools.partial(_select_kernel, cap=cap, chunk=2048),
        out_shape=[
            jax.ShapeDtypeStruct((n_exp, t), F32),
            jax.ShapeDtypeStruct((n_exp, n_tiles), jnp.int32),
            jax.ShapeDtypeStruct((n_exp, n_tiles), jnp.int32),
        ],
        compiler_params=pltpu.CompilerParams(vmem_limit_bytes=VMEM_LIMIT),
        name="select",
    )(aff_t)


def _pack_bf16_pair(x):
    half = x.shape[1] // 2
    hi = lax.bitcast_convert_type(x[:, :half], jnp.uint32) & jnp.uint32(0xFFFF0000)
    lo = lax.shift_right_logical(lax.bitcast_convert_type(x[:, half:], jnp.uint32), jnp.uint32(16))
    return hi | lo


def _unpack_bf16_pair(u):
    a = lax.bitcast_convert_type(u & jnp.uint32(0xFFFF0000), F32)
    b = lax.bitcast_convert_type(lax.shift_left(u, jnp.uint32(16)), F32)
    return a.astype(BF16), b.astype(BF16)


def _scalar_max(ref, i):
    m = ref[0, i]
    for e in range(1, N_EXPERTS):
        m = jnp.maximum(m, ref[e, i])
    return m


def _window_copies(ref_of, rows_ref, offs_ref, i, p, buf, sem, to_hbm, only_if_needed):
    copies = []
    for e in range(N_EXPERTS):
        start = pl.multiple_of(offs_ref[e, i] + p * MOE_WIN, ROW_ALIGN)
        hbm = ref_of.at[e, pl.ds(start, MOE_WIN)]
        vmem = buf.at[pl.ds(e * MOE_WIN, MOE_WIN)]
        copy = pltpu.make_async_copy(vmem, hbm, sem) if to_hbm else pltpu.make_async_copy(hbm, vmem, sem)
        needed = rows_ref[e, i] > p * MOE_WIN
        copies.append((copy, needed if only_if_needed else None))
    return copies


def _start_all(copies):
    for copy, cond in copies:
        if cond is None:
            copy.start()
        else:
            pl.when(cond)(copy.start)


def _wait_all(copies):
    for copy, cond in copies:
        if cond is None:
            copy.wait()
        else:
            pl.when(cond)(copy.wait)


def _dispatch_kernel(rows_ref, offs_ref, gt_ref, xb_ref, xe_ref, stack_ref, zero_ref, sem, zero_sem):
    i = pl.program_id(0)
    n_tiles = pl.num_programs(0)
    slot = i % 2
    n_rows = N_EXPERTS * MOE_WIN
    sel_t = jnp.where(gt_ref[...] > 0.0, 1.0, 0.0).astype(BF16)
    earlier = (lax.broadcasted_iota(jnp.int32, (MOE_TILE, MOE_TILE), 0)
               < lax.broadcasted_iota(jnp.int32, (MOE_TILE, MOE_TILE), 1))
    rank_t = jnp.dot(sel_t, jnp.where(earlier, 1.0, 0.0).astype(BF16), preferred_element_type=F32)
    spread = jnp.where(lax.broadcasted_iota(jnp.int32, (n_rows, N_EXPERTS), 0) // MOE_WIN
                       == lax.broadcasted_iota(jnp.int32, (n_rows, N_EXPERTS), 1), 1.0, 0.0).astype(BF16)
    rank_rows = jnp.dot(spread, rank_t.astype(BF16), preferred_element_type=F32)
    sel_rows = jnp.dot(spread, sel_t, preferred_element_type=F32)
    slot_in_win = lax.broadcasted_iota(jnp.int32, (n_rows, 1), 0) % MOE_WIN

    def fill(p):
        want = (slot_in_win + p * MOE_WIN).astype(F32)
        onehot = jnp.where((rank_rows == want) & (sel_rows > 0.0), 1.0, 0.0).astype(BF16)
        picked = jnp.dot(onehot, xb_ref[...], preferred_element_type=F32)
        stack_ref[slot] = _pack_bf16_pair(picked)

    copies = lambda tile, p, s, only_if_needed: _window_copies(
        xe_ref, rows_ref, offs_ref, tile, p, stack_ref.at[s], sem.at[s], True, only_if_needed)
    passes = lambda tile: (_scalar_max(rows_ref, tile) + MOE_WIN - 1) // MOE_WIN

    fill(0)
    prev = jnp.maximum(i - 1, 0)

    @pl.when((i > 0) & (passes(prev) <= 1))
    def _():
        _wait_all(copies(prev, 0, 1 - slot, False))

    _start_all(copies(i, 0, slot, False))
    n_pass = passes(i)

    @pl.when(n_pass > 1)
    def _():
        _wait_all(copies(i, 0, slot, False))

        def extra(p, carry):
            fill(p)
            _start_all(copies(i, p, slot, True))
            _wait_all(copies(i, p, slot, True))
            return carry
        lax.fori_loop(1, n_pass, extra, 0)

    @pl.when((i == n_tiles - 1) & (n_pass <= 1))
    def _():
        _wait_all(copies(i, 0, slot, False))

    @pl.when(i == n_tiles - 1)
    def _():
        cap_rows = xe_ref.shape[1]
        zero_ref[...] = jnp.zeros(zero_ref.shape, jnp.uint32)
        for e in range(N_EXPERTS):
            total = offs_ref[e, i] + rows_ref[e, i]
            n_full = (cap_rows - total) // MOE_WIN

            def zero_copy(start, e=e):
                return pltpu.make_async_copy(
                    zero_ref, xe_ref.at[e, pl.ds(pl.multiple_of(start, ROW_ALIGN), MOE_WIN)], zero_sem)

            def fill_start(k, carry, total=total, zero_copy=zero_copy):
                zero_copy(total + k * MOE_WIN).start()
                return carry

            def fill_wait(k, carry, zero_copy=zero_copy):
                zero_copy(0).wait()
                return carry

            lax.fori_loop(0, n_full, fill_start, 0)
            lax.fori_loop(0, n_full, fill_wait, 0)
            zero_copy(cap_rows - MOE_WIN).start()
            zero_copy(cap_rows - MOE_WIN).wait()


def _dispatch_call(rows, offs, gates_t, x1b, cap_rows):
    n_exp, t = gates_t.shape
    n_tiles = t // MOE_TILE
    grid_spec = pltpu.PrefetchScalarGridSpec(
        num_scalar_prefetch=2,
        grid=(n_tiles,),
        in_specs=[
            pl.BlockSpec((n_exp, MOE_TILE), lambda i, rows, offs: (0, i)),
            pl.BlockSpec((MOE_TILE, D_MODEL), lambda i, rows, offs: (i, 0)),
        ],
        out_specs=pl.BlockSpec(memory_space=pl.ANY),
        scratch_shapes=[
            pltpu.VMEM((2, n_exp * MOE_WIN, D_MODEL // 2), jnp.uint32),
            pltpu.VMEM((MOE_WIN, D_MODEL // 2), jnp.uint32),
            pltpu.SemaphoreType.DMA((2,)),
            pltpu.SemaphoreType.DMA(()),
        ],
    )
    return pl.pallas_call(
        _dispatch_kernel,
        grid_spec=grid_spec,
        out_shape=jax.ShapeDtypeStruct((n_exp, cap_rows, D_MODEL // 2), jnp.uint32),
        compiler_params=pltpu.CompilerParams(
            dimension_semantics=("arbitrary",), vmem_limit_bytes=VMEM_LIMIT),
        name="dispatch",
    )(rows, offs, gates_t, x1b)


def _ffn_kernel(total_ref, xe_ref, wg_ref, wu_ref, wd_ref, ye_ref):
    e = pl.program_id(0)
    blk = pl.program_id(1)
    n_blk_rows = xe_ref.shape[1]
    total = total_ref[e]

    @pl.when(blk * n_blk_rows < total)
    def _():
        xa, xb = _unpack_bf16_pair(xe_ref[0])
        half = D_MODEL // 2
        hg = (jnp.dot(xa, wg_ref[0, :half, :], preferred_element_type=F32)
              + jnp.dot(xb, wg_ref[0, half:, :], preferred_element_type=F32))
        hu = (jnp.dot(xa, wu_ref[0, :half, :], preferred_element_type=F32)
              + jnp.dot(xb, wu_ref[0, half:, :], preferred_element_type=F32))
        h = hg * jax.nn.sigmoid(hg) * hu
        ye = jnp.dot(h.astype(BF16), wd_ref[0], preferred_element_type=F32)
        ye_ref[0] = _pack_bf16_pair(ye.astype(BF16).astype(F32))

    @pl.when(blk * n_blk_rows >= total)
    def _():
        ye_ref[...] = jnp.zeros(ye_ref.shape, jnp.uint32)


def _ffn_call(totals, xe, p):
    n_exp, cap_rows, half = xe.shape
    n_blk = cap_rows // FFN_ROWS

    def xe_index(e, b, totals):
        last = jnp.maximum((totals[e] + FFN_ROWS - 1) // FFN_ROWS - 1, 0)
        return (e, jnp.minimum(b, last), 0)

    grid_spec = pltpu.PrefetchScalarGridSpec(
        num_scalar_prefetch=1,
        grid=(n_exp, n_blk),
        in_specs=[
            pl.BlockSpec((1, FFN_ROWS, half), xe_index),
            pl.BlockSpec((1, D_MODEL, FFN_HIDDEN), lambda e, b, totals: (e, 0, 0)),
            pl.BlockSpec((1, D_MODEL, FFN_HIDDEN), lambda e, b, totals: (e, 0, 0)),
            pl.BlockSpec((1, FFN_HIDDEN, D_MODEL), lambda e, b, totals: (e, 0, 0)),
        ],
        out_specs=pl.BlockSpec((1, FFN_ROWS, half), lambda e, b, totals: (e, b, 0)),
    )
    return pl.pallas_call(
        _ffn_kernel,
        grid_spec=grid_spec,
        out_shape=jax.ShapeDtypeStruct(xe.shape, jnp.uint32),
        compiler_params=pltpu.CompilerParams(
            dimension_semantics=("arbitrary", "arbitrary"), vmem_limit_bytes=VMEM_LIMIT),
        name="ffn",
    )(totals, xe, p["w_gate"], p["w_up"], p["w_down"])


def _combine_kernel(rows_ref, offs_ref, g_ref, x1_ref, g2_ref, b2_ref, ye_ref, o_ref, land_ref, moe_ref, sem):
    i = pl.program_id(0)
    n_tiles = pl.num_programs(0)
    slot = i % 2
    n_rows = N_EXPERTS * MOE_WIN
    half = D_MODEL // 2
    fetch = lambda tile, p, s, needed: _window_copies(ye_ref, rows_ref, offs_ref, tile, p, land_ref.at[s],
                                                      sem.at[s], False, needed)

    @pl.when(i == 0)
    def _():
        _start_all(fetch(0, 0, 0, False))

    @pl.when(i + 1 < n_tiles)
    def _():
        _start_all(fetch(jnp.minimum(i + 1, n_tiles - 1), 0, 1 - slot, False))

    gates = g_ref[...]
    sel = jnp.where(gates > 0.0, 1.0, 0.0).astype(BF16)
    earlier = (lax.broadcasted_iota(jnp.int32, (MOE_TILE, MOE_TILE), 1)
               < lax.broadcasted_iota(jnp.int32, (MOE_TILE, MOE_TILE), 0))
    rank = jnp.dot(jnp.where(earlier, 1.0, 0.0).astype(BF16), sel, preferred_element_type=F32)
    spread = jnp.where(lax.broadcasted_iota(jnp.int32, (N_EXPERTS, n_rows), 1) // MOE_WIN
                       == lax.broadcasted_iota(jnp.int32, (N_EXPERTS, n_rows), 0), 1.0, 0.0).astype(BF16)
    rank_cols = jnp.dot(rank.astype(BF16), spread, preferred_element_type=F32)
    gate_cols = jnp.dot(gates.astype(BF16), spread, preferred_element_type=F32)
    slot_in_win = lax.broadcasted_iota(jnp.int32, (1, n_rows), 1) % MOE_WIN
    row_expert = lax.broadcasted_iota(jnp.int32, (n_rows, 1), 0) // MOE_WIN
    row_in_win = lax.broadcasted_iota(jnp.int32, (n_rows, 1), 0) % MOE_WIN
    seg_rows = jnp.zeros((n_rows, 1), jnp.int32)
    for e in range(N_EXPERTS):
        seg_rows = jnp.where(row_expert == e, rows_ref[e, i], seg_rows)

    def gather(p, s):
        want = (slot_in_win + p * MOE_WIN).astype(F32)
        weights = jnp.where(rank_cols == want, gate_cols, 0.0).astype(BF16)
        landed = jnp.where(row_in_win + p * MOE_WIN < seg_rows, land_ref[s], jnp.uint32(0))
        ya, yb = _unpack_bf16_pair(landed)
        return (jnp.dot(weights, ya, preferred_element_type=F32),
                jnp.dot(weights, yb, preferred_element_type=F32))

    _wait_all(fetch(i, 0, slot, False))
    ma, mb = gather(0, slot)
    moe_ref[:, :half] = ma
    moe_ref[:, half:] = mb
    n_pass = (_scalar_max(rows_ref, i) + MOE_WIN - 1) // MOE_WIN

    @pl.when(n_pass > 1)
    def _():
        def extra(p, carry):
            _start_all(fetch(i, p, 2, True))
            _wait_all(fetch(i, p, 2, True))
            ea, eb = gather(p, 2)
            moe_ref[:, :half] += ea
            moe_ref[:, half:] += eb
            return carry
        lax.fori_loop(1, n_pass, extra, 0)

    o_ref[...] = _layer_norm(DEEPNORM_ALPHA * x1_ref[...] + moe_ref[...], g2_ref[...], b2_ref[...])


def _combine_call(rows, offs, gates, x1, ye, p):
    t, n_exp = gates.shape
    n_tiles = t // MOE_TILE
    full = lambda shape: pl.BlockSpec(shape, lambda i, rows, offs: (0,) * len(shape))
    grid_spec = pltpu.PrefetchScalarGridSpec(
        num_scalar_prefetch=2,
        grid=(n_tiles,),
        in_specs=[
            pl.BlockSpec((MOE_TILE, n_exp), lambda i, rows, offs: (i, 0)),
            pl.BlockSpec((MOE_TILE, D_MODEL), lambda i, rows, offs: (i, 0)),
            full((1, D_MODEL)), full((1, D_MODEL)),
            pl.BlockSpec(memory_space=pl.ANY),
        ],
        out_specs=pl.BlockSpec((MOE_TILE, D_MODEL), lambda i, rows, offs: (i, 0)),
        scratch_shapes=[
            pltpu.VMEM((3, n_exp * MOE_WIN, D_MODEL // 2), jnp.uint32),
            pltpu.VMEM((MOE_TILE, D_MODEL), F32),
            pltpu.SemaphoreType.DMA((3,)),
        ],
    )
    return pl.pallas_call(
        _combine_kernel,
        grid_spec=grid_spec,
        out_shape=jax.ShapeDtypeStruct((t, D_MODEL), F32),
        compiler_params=pltpu.CompilerParams(
            dimension_semantics=("arbitrary",), vmem_limit_bytes=VMEM_LIMIT),
        name="combine",
    )(rows, offs, gates, x1, p["ln2_g"], p["ln2_b"], ye)


def _rope_tables(s):
    half = QK_ROPE_DIM // 2
    inv = 1.0 / (ROPE_THETA ** (jnp.arange(0, QK_ROPE_DIM, 2, dtype=F32) / QK_ROPE_DIM))
    ang = jnp.arange(s, dtype=F32)[:, None] * inv[None, :]
    cos, sin = jnp.cos(ang), jnp.sin(ang)
    tail = HEAD_PAD - QK_NOPE_DIM - QK_ROPE_DIM
    cos_t = jnp.concatenate([jnp.ones((s, QK_NOPE_DIM), F32), cos, cos, jnp.ones((s, tail), F32)], axis=1)
    sin_t = jnp.concatenate([jnp.zeros((s, QK_NOPE_DIM), F32), -sin, sin, jnp.zeros((s, tail), F32)], axis=1)
    assert half * 2 == QK_ROPE_DIM
    return cos_t, sin_t


def _prepare_params(seq_lens, ln_in_g, ln_in_b, w_in, w_pool, pool_scale, q_norm_g, w_uq, kv_norm_g, w_ukv,
                    w_o, ln1_g, ln1_b, w_router, w_gate, w_up, w_down, ln2_g, ln2_b):
    row = lambda a: a.reshape(1, -1).astype(F32)
    o3 = POOL_DIM + Q_LORA_RANK + KV_LORA_RANK
    w_in0 = w_in[0]
    w_in_p = jnp.concatenate([
        w_in0[:, :o3], jnp.zeros((D_MODEL, QK_NOPE_DIM), F32), w_in0[:, o3:],
        jnp.zeros((D_MODEL, HEAD_PAD - QK_NOPE_DIM - QK_ROPE_DIM), F32)], axis=1)
    qk = QK_NOPE_DIM + QK_ROPE_DIM
    w_uq_p = jnp.pad(w_uq[0].reshape(Q_LORA_RANK, N_HEADS, qk), ((0, 0), (0, 0), (0, HEAD_PAD - qk)))
    w_ukv0 = w_ukv[0].reshape(KV_LORA_RANK, N_HEADS, QK_NOPE_DIM + V_HEAD_DIM)
    w_k = jnp.pad(w_ukv0[:, :, :QK_NOPE_DIM], ((0, 0), (0, 0), (0, HEAD_PAD - QK_NOPE_DIM)))
    w_v = jnp.pad(w_ukv0[:, :, QK_NOPE_DIM:], ((0, 0), (0, 0), (0, V_ROWS - V_HEAD_DIM)))
    tables = {s: _rope_tables(s) for s in seq_lens}
    return {
        "ln_in_g": row(ln_in_g), "ln_in_b": row(ln_in_b),
        "w_in": w_in_p.astype(BF16),
        "w_pool": w_pool[0].astype(BF16), "pool_scale": row(pool_scale[0]),
        "q_norm_g": row(q_norm_g[0]), "w_uq": w_uq_p.reshape(Q_LORA_RANK, N_HEADS * HEAD_PAD).astype(BF16),
        "kv_norm_g": row(kv_norm_g[0]),
        "w_k": w_k.reshape(KV_LORA_RANK, N_HEADS * HEAD_PAD).astype(BF16),
        "w_vt": w_v.reshape(KV_LORA_RANK, N_HEADS * V_ROWS).T.astype(BF16),
        "w_o": w_o[0].astype(BF16),
        "ln1_g": row(ln1_g[0]), "ln1_b": row(ln1_b[0]),
        "w_router": jnp.stack([w_router[0].astype(BF16),
                               (w_router[0] - w_router[0].astype(BF16).astype(F32)).astype(BF16)]),
        "w_gate": w_gate[0].astype(BF16), "w_up": w_up[0].astype(BF16), "w_down": w_down[0].astype(BF16),
        "ln2_g": row(ln2_g[0]), "ln2_b": row(ln2_b[0]),
        "cos": {s: t[0] for s, t in tables.items()},
        "sin": {s: t[1] for s, t in tables.items()},
    }


def _trunk(x, p):
    b, s, d = x.shape
    t = b * s
    u, q, k, vt = _proj_call(x, p, tm=1024)
    attn = _attn_call(q, k, vt, tq=512, tk=512)
    x1, x1b, aff = _mix_call(x, u, attn, p, tm=1024)
    cap = max(1, (CAPACITY_FACTOR * t) // N_EXPERTS)
    gates_t, rows, offs = _select_call(aff.reshape(t, N_EXPERTS).T, cap)
    cap_rows = -(-(cap + ROW_ALIGN * (t // MOE_TILE) + MOE_WIN) // FFN_ROWS) * FFN_ROWS
    xe = _dispatch_call(rows, offs, gates_t, x1b.reshape(t, d), cap_rows)
    totals = offs[:, -1] + rows[:, -1]
    ye = _ffn_call(totals, xe, p)
    y = _combine_call(rows, offs, gates_t.T, x1.reshape(t, d), ye, p)
    return y.reshape(b, s, d)


def kernel(x_prompt, x_sample, ln_in_g, ln_in_b, w_in, w_pool, pool_scale, q_norm_g, w_uq, kv_norm_g, w_ukv,
           w_o, ln1_g, ln1_b, w_router, w_gate, w_up, w_down, ln2_g, ln2_b):
    assert w_in.shape[0] == DEPTH == 1
    p = _prepare_params((x_prompt.shape[1], x_sample.shape[1]), ln_in_g, ln_in_b, w_in, w_pool, pool_scale,
                        q_norm_g, w_uq, kv_norm_g, w_ukv, w_o, ln1_g, ln1_b, w_router, w_gate, w_up, w_down,
                        ln2_g, ln2_b)
    return (_trunk(x_prompt, p), _trunk(x_sample, p))
```

```python
import functools
import math

import jax
import jax.numpy as jnp
from jax import lax
from jax.experimental import pallas as pl
from jax.experimental.pallas import tpu as pltpu

D_MODEL = 1024
POOL_WINDOWS = (2, 4, 8, 16)
POOL_DIM = 512
POOL_GROUP_DIM = 128
N_HEADS = 8
QK_NOPE_DIM = 64
QK_ROPE_DIM = 32
V_HEAD_DIM = 64
ATTN_DIM = N_HEADS * V_HEAD_DIM
Q_LORA_RANK = 384
KV_LORA_RANK = 256
ROPE_THETA = 10000.0
N_EXPERTS = 16
CAPACITY_FACTOR = 2
FFN_HIDDEN = 512
LN_EPS = 1e-5
RMS_EPS = 1e-6
DEPTH = 1
DEEPNORM_ALPHA = (2.0 * DEPTH) ** 0.25

LANES = 128
HEAD_PAD = 128
V_ROWS = HEAD_PAD
HEADS_PER_STEP = 2
MOE_TILE = 256
MOE_WIN = 64
ROW_ALIGN = 8
FFN_ROWS = 1024
HALO = 8
VMEM_LIMIT = 56 * 1024 * 1024

BF16 = jnp.bfloat16
F32 = jnp.float32


def _layer_norm(x, g, b):
    mu = jnp.mean(x, axis=-1, keepdims=True)
    xc = x - mu
    var = jnp.mean(xc * xc, axis=-1, keepdims=True)
    return xc * lax.rsqrt(var + LN_EPS) * g + b


def _rms_norm(x, g):
    return x * lax.rsqrt(jnp.mean(x * x, axis=-1, keepdims=True) + RMS_EPS) * g


def _proj_kernel(x_ref, lng_ref, lnb_ref, win_ref, qg_ref, wuq_ref, kvg_ref, wk_ref, wvt_ref,
                 cos_ref, sin_ref, u_ref, q_ref, k_ref, vt_ref):
    xn = _layer_norm(x_ref[0], lng_ref[...], lnb_ref[...])
    proj = jnp.dot(xn.astype(BF16), win_ref[...], preferred_element_type=F32)
    o1 = POOL_DIM
    o2 = o1 + Q_LORA_RANK
    o3 = o2 + KV_LORA_RANK
    u_ref[0] = proj[:, :o1]
    qn = _rms_norm(proj[:, o1:o2], qg_ref[...])
    q = jnp.dot(qn.astype(BF16), wuq_ref[...], preferred_element_type=F32)
    kvn = _rms_norm(proj[:, o2:o3], kvg_ref[...]).astype(BF16)
    kk = jnp.dot(kvn, wk_ref[...], preferred_element_type=F32)
    vvt = lax.dot_general(wvt_ref[...], kvn, (((1,), (1,)), ((), ())),
                          preferred_element_type=F32)
    cos = cos_ref[...]
    sin = sin_ref[...]
    lane = lax.broadcasted_iota(jnp.int32, cos.shape, 1)
    first_half = lane < QK_NOPE_DIM + QK_ROPE_DIM // 2

    def rope(t):
        partner = jnp.where(first_half,
                            pltpu.roll(t, HEAD_PAD - QK_ROPE_DIM // 2, 1),
                            pltpu.roll(t, QK_ROPE_DIM // 2, 1))
        return t * cos + partner * sin

    k_rope = rope(proj[:, o3:o3 + HEAD_PAD])
    row = lax.broadcasted_iota(jnp.int32, (V_ROWS, vvt.shape[1]), 0)
    ones_rows = jnp.where(row >= V_HEAD_DIM, 1.0, 0.0).astype(F32)
    scale = math.log2(math.e) / math.sqrt(QK_NOPE_DIM + QK_ROPE_DIM)
    for h in range(N_HEADS):
        sl = slice(h * HEAD_PAD, (h + 1) * HEAD_PAD)
        q_ref[0, h] = (rope(q[:, sl]) * scale).astype(BF16)
        k_ref[0, h] = (kk[:, sl] + k_rope).astype(BF16)
        vt_ref[0, h] = (vvt[h * V_ROWS:(h + 1) * V_ROWS, :] + ones_rows).astype(BF16)


def _proj_call(x, p, tm):
    b, s, _ = x.shape
    full = lambda shape: pl.BlockSpec(shape, lambda i, j: (0,) * len(shape))
    hs = pl.BlockSpec((1, N_HEADS, tm, HEAD_PAD), lambda i, j: (i, 0, j, 0))
    return pl.pallas_call(
        _proj_kernel,
        grid=(b, s // tm),
        in_specs=[
            pl.BlockSpec((1, tm, D_MODEL), lambda i, j: (i, j, 0)),
            full((1, D_MODEL)), full((1, D_MODEL)),
            full(p["w_in"].shape),
            full((1, Q_LORA_RANK)), full(p["w_uq"].shape),
            full((1, KV_LORA_RANK)), full(p["w_k"].shape), full(p["w_vt"].shape),
            pl.BlockSpec((tm, HEAD_PAD), lambda i, j: (j, 0)),
            pl.BlockSpec((tm, HEAD_PAD), lambda i, j: (j, 0)),
        ],
        out_specs=[
            pl.BlockSpec((1, tm, POOL_DIM), lambda i, j: (i, j, 0)),
            hs, hs, pl.BlockSpec((1, N_HEADS, V_ROWS, tm), lambda i, j: (i, 0, 0, j)),
        ],
        out_shape=[
            jax.ShapeDtypeStruct((b, s, POOL_DIM), F32),
            jax.ShapeDtypeStruct((b, N_HEADS, s, HEAD_PAD), BF16),
            jax.ShapeDtypeStruct((b, N_HEADS, s, HEAD_PAD), BF16),
            jax.ShapeDtypeStruct((b, N_HEADS, V_ROWS, s), BF16),
        ],
        compiler_params=pltpu.CompilerParams(
            dimension_semantics=("arbitrary", "arbitrary"), vmem_limit_bytes=VMEM_LIMIT),
        name="proj",
    )(x, p["ln_in_g"], p["ln_in_b"], p["w_in"], p["q_norm_g"], p["w_uq"], p["kv_norm_g"],
      p["w_k"], p["w_vt"], p["cos"][s], p["sin"][s])


def _attn_kernel(q_ref, k_ref, vt_ref, o_ref, m_ref, acc_ref, s_ref, p_ref, alpha_ref, cmax_ref, *, tk):
    s_len = k_ref.shape[2]
    tq = q_ref.shape[2]
    n_chunks = s_len // tk
    m_ref[...] = jnp.full(m_ref.shape, -jnp.inf, F32)
    acc_ref[...] = jnp.zeros(acc_ref.shape, F32)

    def scores(j, slot):
        start = j * tk
        for hh in range(HEADS_PER_STEP):
            k = k_ref[0, hh, pl.ds(start, tk), :]
            st = lax.dot_general(k, q_ref[0, hh], (((1,), (1,)), ((), ())),
                                 preferred_element_type=F32)
            s_ref[slot, hh] = st
            cmax_ref[slot, hh] = jnp.max(st, axis=0, keepdims=True)

    def softmax(slot):
        for hh in range(HEADS_PER_STEP):
            m_old = m_ref[hh]
            m_new = jnp.maximum(m_old, cmax_ref[slot, hh])
            p_ref[slot, hh] = jnp.exp2(s_ref[slot, hh] - m_new).astype(BF16)
            alpha_ref[slot, hh] = jnp.exp2(m_old - m_new)
            m_ref[hh] = m_new

    def values(j, slot):
        start = j * tk
        for hh in range(HEADS_PER_STEP):
            vt = vt_ref[0, hh, :, pl.ds(start, tk)]
            acc_ref[hh] = acc_ref[hh] * alpha_ref[slot, hh] + jnp.dot(
                vt, p_ref[slot, hh], preferred_element_type=F32)

    scores(0, 0)
    for j in range(n_chunks):
        if j + 1 < n_chunks:
            scores(j + 1, (j + 1) % 2)
        if j >= 1:
            values(j - 1, (j - 1) % 2)
        softmax(j % 2)
    values(n_chunks - 1, (n_chunks - 1) % 2)
    heads = []
    for hh in range(HEADS_PER_STEP):
        acc = acc_ref[hh]
        heads.append((acc / acc[V_HEAD_DIM:V_HEAD_DIM + 1, :]).T)
    lane = lax.broadcasted_iota(jnp.int32, (tq, HEAD_PAD), 1)
    out = jnp.where(lane < V_HEAD_DIM, heads[0], pltpu.roll(heads[1], V_HEAD_DIM, 1))
    o_ref[0] = out.astype(BF16)


def _attn_call(q, k, vt, tq, tk):
    b, h, s, _ = q.shape
    hp = HEADS_PER_STEP
    return pl.pallas_call(
        functools.partial(_attn_kernel, tk=tk),
        grid=(b, h // hp, s // tq),
        in_specs=[
            pl.BlockSpec((1, hp, tq, HEAD_PAD), lambda i, j, t: (i, j, t, 0)),
            pl.BlockSpec((1, hp, s, HEAD_PAD), lambda i, j, t: (i, j, 0, 0)),
            pl.BlockSpec((1, hp, V_ROWS, s), lambda i, j, t: (i, j, 0, 0)),
        ],
        out_specs=pl.BlockSpec((1, tq, hp * V_HEAD_DIM), lambda i, j, t: (i, t, j)),
        out_shape=jax.ShapeDtypeStruct((b, s, ATTN_DIM), BF16),
        scratch_shapes=[
            pltpu.VMEM((hp, 1, tq), F32),
            pltpu.VMEM((hp, V_ROWS, tq), F32),
            pltpu.VMEM((2, hp, tk, tq), F32),
            pltpu.VMEM((2, hp, tk, tq), BF16),
            pltpu.VMEM((2, hp, 1, tq), F32),
            pltpu.VMEM((2, hp, 1, tq), F32),
        ],
        compiler_params=pltpu.CompilerParams(
            dimension_semantics=("arbitrary", "arbitrary", "arbitrary"), vmem_limit_bytes=VMEM_LIMIT),
        name="attn",
    )(q, k, vt)


def _mix_kernel(x_ref, up_ref, um_ref, un_ref, at_ref, lng_ref, lnb_ref, wp_ref, ps_ref, wo_ref,
                g1_ref, b1_ref, wr_ref, x1_ref, x1b_ref, aff_ref, ubuf):
    tm = um_ref.shape[1]
    j = pl.program_id(1)
    n_tiles = pl.num_programs(1)
    s_len = n_tiles * tm
    xn = _layer_norm(x_ref[0], lng_ref[...], lnb_ref[...])

    ubuf[pl.ds(0, HALO), :] = jnp.where(j > 0, up_ref[0], 0.0)
    ubuf[pl.ds(HALO, tm), :] = um_ref[0]
    ubuf[pl.ds(HALO + tm, HALO), :] = jnp.where(j < n_tiles - 1, un_ref[0], 0.0)

    edge = lax.broadcasted_iota(jnp.int32, (HALO, 1), 0)

    def inv_count(pos, w):
        lo = jnp.maximum(pos - w // 2, 0)
        hi = jnp.minimum(pos + w // 2 - 1, s_len - 1)
        return 1.0 / (hi - lo + 1).astype(F32)

    mix = jnp.dot(at_ref[0], wo_ref[pl.ds(POOL_DIM, ATTN_DIM), :], preferred_element_type=F32)
    for g, w in enumerate(POOL_WINDOWS):
        cols = pl.ds(g * POOL_GROUP_DIM, POOL_GROUP_DIM)
        n_buf = tm + 2 * HALO
        acc = ubuf[:, cols]
        span = 1
        while span < w // 2:
            acc = acc + pltpu.roll(acc, n_buf - span, 0)
            span *= 2
        acc = acc + pltpu.roll(acc, w // 2, 0)
        win_sum = acc[HALO:HALO + tm]
        u_g = ubuf[pl.ds(HALO, tm), cols]
        d = jnp.concatenate([
            win_sum[:HALO] * inv_count(j * tm + edge, w) - u_g[:HALO],
            win_sum[HALO:tm - HALO] * (1.0 / w) - u_g[HALO:tm - HALO],
            win_sum[tm - HALO:] * inv_count(j * tm + tm - HALO + edge, w) - u_g[tm - HALO:],
        ], axis=0)
        y = jnp.dot(d.astype(BF16), wp_ref[g], preferred_element_type=F32) * ps_ref[:, cols]
        mix = mix + jnp.dot(y.astype(BF16), wo_ref[cols, :], preferred_element_type=F32)

    x1 = _layer_norm(DEEPNORM_ALPHA * xn + mix, g1_ref[...], b1_ref[...])
    x1_ref[0] = x1
    x1_hi = x1.astype(BF16)
    x1b_ref[0] = x1_hi
    x1_lo = (x1 - x1_hi.astype(F32)).astype(BF16)
    logits = (jnp.dot(x1_hi, wr_ref[0], preferred_element_type=F32)
              + jnp.dot(x1_hi, wr_ref[1], preferred_element_type=F32)
              + jnp.dot(x1_lo, wr_ref[0], preferred_element_type=F32))
    e = jnp.exp(logits - jnp.max(logits, axis=-1, keepdims=True))
    aff_ref[0] = e / jnp.sum(e, axis=-1, keepdims=True)


def _mix_call(x, u, attn, p, tm):
    b, s, _ = x.shape
    full = lambda shape: pl.BlockSpec(shape, lambda i, j: (0,) * len(shape))
    tile = lambda width: pl.BlockSpec((1, tm, width), lambda i, j: (i, j, 0))
    hb = tm // HALO
    last_halo_block = s // HALO - 1
    return pl.pallas_call(
        _mix_kernel,
        grid=(b, s // tm),
        in_specs=[
            tile(D_MODEL),
            pl.BlockSpec((1, HALO, POOL_DIM), lambda i, j: (i, jnp.maximum(j * hb - 1, 0), 0)),
            tile(POOL_DIM),
            pl.BlockSpec((1, HALO, POOL_DIM), lambda i, j: (i, jnp.minimum((j + 1) * hb, last_halo_block), 0)),
            tile(ATTN_DIM),
            full((1, D_MODEL)), full((1, D_MODEL)),
            full(p["w_pool"].shape), full((1, POOL_DIM)), full(p["w_o"].shape),
            full((1, D_MODEL)), full((1, D_MODEL)), full(p["w_router"].shape),
        ],
        out_specs=[tile(D_MODEL), tile(D_MODEL), tile(N_EXPERTS)],
        out_shape=[
            jax.ShapeDtypeStruct((b, s, D_MODEL), F32),
            jax.ShapeDtypeStruct((b, s, D_MODEL), BF16),
            jax.ShapeDtypeStruct((b, s, N_EXPERTS), F32),
        ],
        scratch_shapes=[pltpu.VMEM((tm + 2 * HALO, POOL_DIM), F32)],
        compiler_params=pltpu.CompilerParams(
            dimension_semantics=("arbitrary", "arbitrary"), vmem_limit_bytes=VMEM_LIMIT),
        name="mix",
    )(x, u, u, u, attn, p["ln_in_g"], p["ln_in_b"], p["w_pool"], p["pool_scale"], p["w_o"],
      p["ln1_g"], p["ln1_b"], p["w_router"])


def _select_kernel(aff_ref, gate_ref, rows_ref, offs_ref, *, cap, chunk):
    n_exp, t = aff_ref.shape
    n_chunks = t // chunk

    def count(pred):
        def body(c, acc):
            start = pl.multiple_of(c * chunk, chunk)
            bits = lax.bitcast_convert_type(aff_ref[:, pl.ds(start, chunk)], jnp.int32)
            tok = start + lax.broadcasted_iota(jnp.int32, (n_exp, chunk), 1)
            hits = jnp.where(pred(bits, tok), 1.0, 0.0)
            for k in range(chunk // LANES):
                acc = acc + hits[:, k * LANES:(k + 1) * LANES]
            return acc
        partial = lax.fori_loop(0, n_chunks, body, jnp.zeros((n_exp, LANES), F32))
        return jnp.sum(partial, axis=1, keepdims=True)

    def value_step(i, thr):
        cand = thr | (1 << (30 - i))
        n_ge = count(lambda bits, tok: bits >= cand)
        return jnp.where(n_ge >= cap, cand, thr)
    thr = lax.fori_loop(0, 31, value_step, jnp.zeros((n_exp, 1), jnp.int32))

    need = cap - count(lambda bits, tok: bits > thr)
    index_bits = t.bit_length()

    def index_step(i, below):
        cand = below + (1 << (index_bits - 1 - i))
        n_ties = count(lambda bits, tok: (bits == thr) & (tok < cand))
        return jnp.where((cand <= t) & (n_ties < need), cand, below)
    cut = lax.fori_loop(0, index_bits, index_step, jnp.zeros((n_exp, 1), jnp.int32)) + 1

    def write(c, counts):
        start = pl.multiple_of(c * chunk, chunk)
        aff = aff_ref[:, pl.ds(start, chunk)]
        bits = lax.bitcast_convert_type(aff, jnp.int32)
        tok = start + lax.broadcasted_iota(jnp.int32, (n_exp, chunk), 1)
        sel = (bits > thr) | ((bits == thr) & (tok < cut))
        gate = jnp.where(sel, aff, 0.0)
        gate_ref[:, pl.ds(start, chunk)] = gate
        tile_lane = lax.broadcasted_iota(jnp.int32, counts.shape, 1)
        for k in range(chunk // MOE_TILE):
            n = jnp.sum(jnp.where(gate[:, k * MOE_TILE:(k + 1) * MOE_TILE] > 0.0, 1.0, 0.0),
                        axis=1, keepdims=True)
            counts = jnp.where(tile_lane == c * (chunk // MOE_TILE) + k, n, counts)
        return counts
    n_tiles = t // MOE_TILE
    counts = lax.fori_loop(0, n_chunks, write, jnp.zeros((n_exp, n_tiles), F32))

    rows = jnp.floor((counts + (ROW_ALIGN - 1)) * (1.0 / ROW_ALIGN)) * ROW_ALIGN
    before = (lax.broadcasted_iota(jnp.int32, (n_tiles, n_tiles), 0)
              < lax.broadcasted_iota(jnp.int32, (n_tiles, n_tiles), 1))
    offs = jnp.dot(rows.astype(BF16), jnp.where(before, 1.0, 0.0).astype(BF16), preferred_element_type=F32)
    rows_ref[...] = rows.astype(jnp.int32)
    offs_ref[...] = offs.astype(jnp.int32)


def _select_call(aff_t, cap):
    n_exp, t = aff_t.shape
    n_tiles = t // MOE_TILE
    return pl.pallas_call(
        functools.partial(_select_kernel, cap=cap, chunk=2048),
        out_shape=[
            jax.ShapeDtypeStruct((n_exp, t), F32),
            jax.ShapeDtypeStruct((n_exp, n_tiles), jnp.int32),
            jax.ShapeDtypeStruct((n_exp, n_tiles), jnp.int32),
        ],
        compiler_params=pltpu.CompilerParams(vmem_limit_bytes=VMEM_LIMIT),
        name="select",
    )(aff_t)


def _pack_bf16_pair(x):
    half = x.shape[1] // 2
    hi = lax.bitcast_convert_type(x[:, :half], jnp.uint32) & jnp.uint32(0xFFFF0000)
    lo = lax.shift_right_logical(lax.bitcast_convert_type(x[:, half:], jnp.uint32), jnp.uint32(16))
    return hi | lo


def _unpack_bf16_pair(u):
    a = lax.bitcast_convert_type(u & jnp.uint32(0xFFFF0000), F32)
    b = lax.bitcast_convert_type(lax.shift_left(u, jnp.uint32(16)), F32)
    return a.astype(BF16), b.astype(BF16)


def _scalar_max(ref, i):
    m = ref[0, i]
    for e in range(1, N_EXPERTS):
        m = jnp.maximum(m, ref[e, i])
    return m


def _window_copies(ref_of, rows_ref, offs_ref, i, p, buf, sem, to_hbm, only_if_needed):
    copies = []
    for e in range(N_EXPERTS):
        start = pl.multiple_of(offs_ref[e, i] + p * MOE_WIN, ROW_ALIGN)
        hbm = ref_of.at[e, pl.ds(start, MOE_WIN)]
        vmem = buf.at[pl.ds(e * MOE_WIN, MOE_WIN)]
        copy = pltpu.make_async_copy(vmem, hbm, sem) if to_hbm else pltpu.make_async_copy(hbm, vmem, sem)
        needed = rows_ref[e, i] > p * MOE_WIN
        copies.append((copy, needed if only_if_needed else None))
    return copies


def _start_all(copies):
    for copy, cond in copies:
        if cond is None:
            copy.start()
        else:
            pl.when(cond)(copy.start)


def _wait_all(copies):
    for copy, cond in copies:
        if cond is None:
            copy.wait()
        else:
            pl.when(cond)(copy.wait)


def _dispatch_kernel(rows_ref, offs_ref, gt_ref, xb_ref, xe_ref, stack_ref, zero_ref, sem, zero_sem):
    i = pl.program_id(0)
    n_tiles = pl.num_programs(0)
    slot = i % 2
    n_rows = N_EXPERTS * MOE_WIN
    sel_t = jnp.where(gt_ref[...] > 0.0, 1.0, 0.0).astype(BF16)
    earlier = (lax.broadcasted_iota(jnp.int32, (MOE_TILE, MOE_TILE), 0)
               < lax.broadcasted_iota(jnp.int32, (MOE_TILE, MOE_TILE), 1))
    rank_t = jnp.dot(sel_t, jnp.where(earlier, 1.0, 0.0).astype(BF16), preferred_element_type=F32)
    spread = jnp.where(lax.broadcasted_iota(jnp.int32, (n_rows, N_EXPERTS), 0) // MOE_WIN
                       == lax.broadcasted_iota(jnp.int32, (n_rows, N_EXPERTS), 1), 1.0, 0.0).astype(BF16)
    rank_rows = jnp.dot(spread, rank_t.astype(BF16), preferred_element_type=F32)
    sel_rows = jnp.dot(spread, sel_t, preferred_element_type=F32)
    slot_in_win = lax.broadcasted_iota(jnp.int32, (n_rows, 1), 0) % MOE_WIN

    def fill(p):
        want = (slot_in_win + p * MOE_WIN).astype(F32)
        onehot = jnp.where((rank_rows == want) & (sel_rows > 0.0), 1.0, 0.0).astype(BF16)
        picked = jnp.dot(onehot, xb_ref[...], preferred_element_type=F32)
        stack_ref[slot] = _pack_bf16_pair(picked)

    copies = lambda tile, p, s, only_if_needed: _window_copies(
        xe_ref, rows_ref, offs_ref, tile, p, stack_ref.at[s], sem.at[s], True, only_if_needed)
    passes = lambda tile: (_scalar_max(rows_ref, tile) + MOE_WIN - 1) // MOE_WIN

    fill(0)
    prev = jnp.maximum(i - 1, 0)

    @pl.when((i > 0) & (passes(prev) <= 1))
    def _():
        _wait_all(copies(prev, 0, 1 - slot, False))

    _start_all(copies(i, 0, slot, False))
    n_pass = passes(i)

    @pl.when(n_pass > 1)
    def _():
        _wait_all(copies(i, 0, slot, False))

        def extra(p, carry):
            fill(p)
            _start_all(copies(i, p, slot, True))
            _wait_all(copies(i, p, slot, True))
            return carry
        lax.fori_loop(1, n_pass, extra, 0)

    @pl.when((i == n_tiles - 1) & (n_pass <= 1))
    def _():
        _wait_all(copies(i, 0, slot, False))

    @pl.when(i == n_tiles - 1)
    def _():
        cap_rows = xe_ref.shape[1]
        zero_ref[...] = jnp.zeros(zero_ref.shape, jnp.uint32)
        for e in range(N_EXPERTS):
            total = offs_ref[e, i] + rows_ref[e, i]
            n_full = (cap_rows - total) // MOE_WIN

            def zero_copy(start, e=e):
                return pltpu.make_async_copy(
                    zero_ref, xe_ref.at[e, pl.ds(pl.multiple_of(start, ROW_ALIGN), MOE_WIN)], zero_sem)

            def fill_start(k, carry, total=total, zero_copy=zero_copy):
                zero_copy(total + k * MOE_WIN).start()
                return carry

            def fill_wait(k, carry, zero_copy=zero_copy):
                zero_copy(0).wait()
                return carry

            lax.fori_loop(0, n_full, fill_start, 0)
            lax.fori_loop(0, n_full, fill_wait, 0)
            zero_copy(cap_rows - MOE_WIN).start()
            zero_copy(cap_rows - MOE_WIN).wait()


def _dispatch_call(rows, offs, gates_t, x1b, cap_rows):
    n_exp, t = gates_t.shape
    n_tiles = t // MOE_TILE
    grid_spec = pltpu.PrefetchScalarGridSpec(
        num_scalar_prefetch=2,
        grid=(n_tiles,),
        in_specs=[
            pl.BlockSpec((n_exp, MOE_TILE), lambda i, rows, offs: (0, i)),
            pl.BlockSpec((MOE_TILE, D_MODEL), lambda i, rows, offs: (i, 0)),
        ],
        out_specs=pl.BlockSpec(memory_space=pl.ANY),
        scratch_shapes=[
            pltpu.VMEM((2, n_exp * MOE_WIN, D_MODEL // 2), jnp.uint32),
            pltpu.VMEM((MOE_WIN, D_MODEL // 2), jnp.uint32),
            pltpu.SemaphoreType.DMA((2,)),
            pltpu.SemaphoreType.DMA(()),
        ],
    )
    return pl.pallas_call(
        _dispatch_kernel,
        grid_spec=grid_spec,
        out_shape=jax.ShapeDtypeStruct((n_exp, cap_rows, D_MODEL // 2), jnp.uint32),
        compiler_params=pltpu.CompilerParams(
            dimension_semantics=("arbitrary",), vmem_limit_bytes=VMEM_LIMIT),
        name="dispatch",
    )(rows, offs, gates_t, x1b)


def _ffn_kernel(total_ref, xe_ref, wg_ref, wu_ref, wd_ref, ye_ref, wg_bf, wu_bf, wd_bf):
    e = pl.program_id(0)
    blk = pl.program_id(1)
    n_blk_rows = xe_ref.shape[1]
    total = total_ref[e]

    @pl.when(blk == 0)
    def _():
        wg_bf[...] = wg_ref[0].astype(BF16)
        wu_bf[...] = wu_ref[0].astype(BF16)
        wd_bf[...] = wd_ref[0].astype(BF16)

    @pl.when(blk * n_blk_rows < total)
    def _():
        xa, xb = _unpack_bf16_pair(xe_ref[0])
        half = D_MODEL // 2
        hg = (jnp.dot(xa, wg_bf[:half, :], preferred_element_type=F32)
              + jnp.dot(xb, wg_bf[half:, :], preferred_element_type=F32))
        hu = (jnp.dot(xa, wu_bf[:half, :], preferred_element_type=F32)
              + jnp.dot(xb, wu_bf[half:, :], preferred_element_type=F32))
        h = hg * jax.nn.sigmoid(hg) * hu
        ye = jnp.dot(h.astype(BF16), wd_bf[...], preferred_element_type=F32)
        ye_ref[0] = _pack_bf16_pair(ye.astype(BF16).astype(F32))

    @pl.when(blk * n_blk_rows >= total)
    def _():
        ye_ref[...] = jnp.zeros(ye_ref.shape, jnp.uint32)


def _ffn_call(totals, xe, p):
    n_exp, cap_rows, half = xe.shape
    n_blk = cap_rows // FFN_ROWS

    def xe_index(e, b, totals):
        last = jnp.maximum((totals[e] + FFN_ROWS - 1) // FFN_ROWS - 1, 0)
        return (e, jnp.minimum(b, last), 0)

    grid_spec = pltpu.PrefetchScalarGridSpec(
        num_scalar_prefetch=1,
        grid=(n_exp, n_blk),
        in_specs=[
            pl.BlockSpec((1, FFN_ROWS, half), xe_index),
            pl.BlockSpec((1, D_MODEL, FFN_HIDDEN), lambda e, b, totals: (e, 0, 0)),
            pl.BlockSpec((1, D_MODEL, FFN_HIDDEN), lambda e, b, totals: (e, 0, 0)),
            pl.BlockSpec((1, FFN_HIDDEN, D_MODEL), lambda e, b, totals: (e, 0, 0)),
        ],
        out_specs=pl.BlockSpec((1, FFN_ROWS, half), lambda e, b, totals: (e, b, 0)),
        scratch_shapes=[
            pltpu.VMEM((D_MODEL, FFN_HIDDEN), BF16),
            pltpu.VMEM((D_MODEL, FFN_HIDDEN), BF16),
            pltpu.VMEM((FFN_HIDDEN, D_MODEL), BF16),
        ],
    )
    return pl.pallas_call(
        _ffn_kernel,
        grid_spec=grid_spec,
        out_shape=jax.ShapeDtypeStruct(xe.shape, jnp.uint32),
        compiler_params=pltpu.CompilerParams(
            dimension_semantics=("arbitrary", "arbitrary"), vmem_limit_bytes=VMEM_LIMIT),
        name="ffn",
    )(totals, xe, p["w_gate"], p["w_up"], p["w_down"])


def _combine_kernel(rows_ref, offs_ref, g_ref, x1_ref, g2_ref, b2_ref, ye_ref, o_ref, land_ref, moe_ref, sem):
    i = pl.program_id(0)
    n_tiles = pl.num_programs(0)
    slot = i % 2
    n_rows = N_EXPERTS * MOE_WIN
    half = D_MODEL // 2
    fetch = lambda tile, p, s, needed: _window_copies(ye_ref, rows_ref, offs_ref, tile, p, land_ref.at[s],
                                                      sem.at[s], False, needed)

    @pl.when(i == 0)
    def _():
        _start_all(fetch(0, 0, 0, False))

    @pl.when(i + 1 < n_tiles)
    def _():
        _start_all(fetch(jnp.minimum(i + 1, n_tiles - 1), 0, 1 - slot, False))

    gates = g_ref[...]
    sel = jnp.where(gates > 0.0, 1.0, 0.0).astype(BF16)
    earlier = (lax.broadcasted_iota(jnp.int32, (MOE_TILE, MOE_TILE), 1)
               < lax.broadcasted_iota(jnp.int32, (MOE_TILE, MOE_TILE), 0))
    rank = jnp.dot(jnp.where(earlier, 1.0, 0.0).astype(BF16), sel, preferred_element_type=F32)
    spread = jnp.where(lax.broadcasted_iota(jnp.int32, (N_EXPERTS, n_rows), 1) // MOE_WIN
                       == lax.broadcasted_iota(jnp.int32, (N_EXPERTS, n_rows), 0), 1.0, 0.0).astype(BF16)
    rank_cols = jnp.dot(rank.astype(BF16), spread, preferred_element_type=F32)
    gate_cols = jnp.dot(gates.astype(BF16), spread, preferred_element_type=F32)
    slot_in_win = lax.broadcasted_iota(jnp.int32, (1, n_rows), 1) % MOE_WIN
    row_expert = lax.broadcasted_iota(jnp.int32, (n_rows, 1), 0) // MOE_WIN
    row_in_win = lax.broadcasted_iota(jnp.int32, (n_rows, 1), 0) % MOE_WIN
    seg_rows = jnp.zeros((n_rows, 1), jnp.int32)
    for e in range(N_EXPERTS):
        seg_rows = jnp.where(row_expert == e, rows_ref[e, i], seg_rows)

    def gather(p, s):
        want = (slot_in_win + p * MOE_WIN).astype(F32)
        weights = jnp.where(rank_cols == want, gate_cols, 0.0).astype(BF16)
        landed = jnp.where(row_in_win + p * MOE_WIN < seg_rows, land_ref[s], jnp.uint32(0))
        ya, yb = _unpack_bf16_pair(landed)
        return (jnp.dot(weights, ya, preferred_element_type=F32),
                jnp.dot(weights, yb, preferred_element_type=F32))

    _wait_all(fetch(i, 0, slot, False))
    ma, mb = gather(0, slot)
    moe_ref[:, :half] = ma
    moe_ref[:, half:] = mb
    n_pass = (_scalar_max(rows_ref, i) + MOE_WIN - 1) // MOE_WIN

    @pl.when(n_pass > 1)
    def _():
        def extra(p, carry):
            _start_all(fetch(i, p, 2, True))
            _wait_all(fetch(i, p, 2, True))
            ea, eb = gather(p, 2)
            moe_ref[:, :half] += ea
            moe_ref[:, half:] += eb
            return carry
        lax.fori_loop(1, n_pass, extra, 0)

    o_ref[...] = _layer_norm(DEEPNORM_ALPHA * x1_ref[...] + moe_ref[...], g2_ref[...], b2_ref[...])


def _combine_call(rows, offs, gates, x1, ye, p):
    t, n_exp = gates.shape
    n_tiles = t // MOE_TILE
    full = lambda shape: pl.BlockSpec(shape, lambda i, rows, offs: (0,) * len(shape))
    grid_spec = pltpu.PrefetchScalarGridSpec(
        num_scalar_prefetch=2,
        grid=(n_tiles,),
        in_specs=[
            pl.BlockSpec((MOE_TILE, n_exp), lambda i, rows, offs: (i, 0)),
            pl.BlockSpec((MOE_TILE, D_MODEL), lambda i, rows, offs: (i, 0)),
            full((1, D_MODEL)), full((1, D_MODEL)),
            pl.BlockSpec(memory_space=pl.ANY),
        ],
        out_specs=pl.BlockSpec((MOE_TILE, D_MODEL), lambda i, rows, offs: (i, 0)),
        scratch_shapes=[
            pltpu.VMEM((3, n_exp * MOE_WIN, D_MODEL // 2), jnp.uint32),
            pltpu.VMEM((MOE_TILE, D_MODEL), F32),
            pltpu.SemaphoreType.DMA((3,)),
        ],
    )
    return pl.pallas_call(
        _combine_kernel,
        grid_spec=grid_spec,
        out_shape=jax.ShapeDtypeStruct((t, D_MODEL), F32),
        compiler_params=pltpu.CompilerParams(
            dimension_semantics=("arbitrary",), vmem_limit_bytes=VMEM_LIMIT),
        name="combine",
    )(rows, offs, gates, x1, p["ln2_g"], p["ln2_b"], ye)


def _rope_tables(s):
    half = QK_ROPE_DIM // 2
    inv = 1.0 / (ROPE_THETA ** (jnp.arange(0, QK_ROPE_DIM, 2, dtype=F32) / QK_ROPE_DIM))
    ang = jnp.arange(s, dtype=F32)[:, None] * inv[None, :]
    cos, sin = jnp.cos(ang), jnp.sin(ang)
    tail = HEAD_PAD - QK_NOPE_DIM - QK_ROPE_DIM
    cos_t = jnp.concatenate([jnp.ones((s, QK_NOPE_DIM), F32), cos, cos, jnp.ones((s, tail), F32)], axis=1)
    sin_t = jnp.concatenate([jnp.zeros((s, QK_NOPE_DIM), F32), -sin, sin, jnp.zeros((s, tail), F32)], axis=1)
    assert half * 2 == QK_ROPE_DIM
    return cos_t, sin_t


def _prepare_params(seq_lens, ln_in_g, ln_in_b, w_in, w_pool, pool_scale, q_norm_g, w_uq, kv_norm_g, w_ukv,
                    w_o, ln1_g, ln1_b, w_router, w_gate, w_up, w_down, ln2_g, ln2_b):
    row = lambda a: a.reshape(1, -1).astype(F32)
    o3 = POOL_DIM + Q_LORA_RANK + KV_LORA_RANK
    w_in0 = w_in[0]
    w_in_p = jnp.concatenate([
        w_in0[:, :o3], jnp.zeros((D_MODEL, QK_NOPE_DIM), F32), w_in0[:, o3:],
        jnp.zeros((D_MODEL, HEAD_PAD - QK_NOPE_DIM - QK_ROPE_DIM), F32)], axis=1)
    qk = QK_NOPE_DIM + QK_ROPE_DIM
    w_uq_p = jnp.pad(w_uq[0].reshape(Q_LORA_RANK, N_HEADS, qk), ((0, 0), (0, 0), (0, HEAD_PAD - qk)))
    w_ukv0 = w_ukv[0].reshape(KV_LORA_RANK, N_HEADS, QK_NOPE_DIM + V_HEAD_DIM)
    w_k = jnp.pad(w_ukv0[:, :, :QK_NOPE_DIM], ((0, 0), (0, 0), (0, HEAD_PAD - QK_NOPE_DIM)))
    w_v = jnp.pad(w_ukv0[:, :, QK_NOPE_DIM:], ((0, 0), (0, 0), (0, V_ROWS - V_HEAD_DIM)))
    tables = {s: _rope_tables(s) for s in seq_lens}
    return {
        "ln_in_g": row(ln_in_g), "ln_in_b": row(ln_in_b),
        "w_in": w_in_p.astype(BF16),
        "w_pool": w_pool[0].astype(BF16), "pool_scale": row(pool_scale[0]),
        "q_norm_g": row(q_norm_g[0]), "w_uq": w_uq_p.reshape(Q_LORA_RANK, N_HEADS * HEAD_PAD).astype(BF16),
        "kv_norm_g": row(kv_norm_g[0]),
        "w_k": w_k.reshape(KV_LORA_RANK, N_HEADS * HEAD_PAD).astype(BF16),
        "w_vt": w_v.reshape(KV_LORA_RANK, N_HEADS * V_ROWS).T.astype(BF16),
        "w_o": w_o[0].astype(BF16),
        "ln1_g": row(ln1_g[0]), "ln1_b": row(ln1_b[0]),
        "w_router": jnp.stack([w_router[0].astype(BF16),
                               (w_router[0] - w_router[0].astype(BF16).astype(F32)).astype(BF16)]),
        "w_gate": w_gate[0], "w_up": w_up[0], "w_down": w_down[0],
        "ln2_g": row(ln2_g[0]), "ln2_b": row(ln2_b[0]),
        "cos": {s: t[0] for s, t in tables.items()},
        "sin": {s: t[1] for s, t in tables.items()},
    }


def _trunk(x, p):
    b, s, d = x.shape
    t = b * s
    u, q, k, vt = _proj_call(x, p, tm=1024)
    attn = _attn_call(q, k, vt, tq=1024, tk=512)
    x1, x1b, aff = _mix_call(x, u, attn, p, tm=1024)
    cap = max(1, (CAPACITY_FACTOR * t) // N_EXPERTS)
    gates_t, rows, offs = _select_call(aff.reshape(t, N_EXPERTS).T, cap)
    cap_rows = -(-(cap + ROW_ALIGN * (t // MOE_TILE) + MOE_WIN) // FFN_ROWS) * FFN_ROWS
    xe = _dispatch_call(rows, offs, gates_t, x1b.reshape(t, d), cap_rows)
    totals = offs[:, -1] + rows[:, -1]
    ye = _ffn_call(totals, xe, p)
    y = _combine_call(rows, offs, gates_t.T, x1.reshape(t, d), ye, p)
    return y.reshape(b, s, d)


def kernel(x_prompt, x_sample, ln_in_g, ln_in_b, w_in, w_pool, pool_scale, q_norm_g, w_uq, kv_norm_g, w_ukv,
           w_o, ln1_g, ln1_b, w_router, w_gate, w_up, w_down, ln2_g, ln2_b):
    assert w_in.shape[0] == DEPTH == 1
    p = _prepare_params((x_prompt.shape[1], x_sample.shape[1]), ln_in_g, ln_in_b, w_in, w_pool, pool_scale,
                        q_norm_g, w_uq, kv_norm_g, w_ukv, w_o, ln1_g, ln1_b, w_router, w_gate, w_up, w_down,
                        ln2_g, ln2_b)
    return (_trunk(x_prompt, p), _trunk(x_sample, p))
```

```python
import functools
import math

import jax
import jax.numpy as jnp
from jax import lax
from jax.experimental import pallas as pl
from jax.experimental.pallas import tpu as pltpu

D_MODEL = 1024
POOL_WINDOWS = (2, 4, 8, 16)
POOL_DIM = 512
POOL_GROUP_DIM = 128
N_HEADS = 8
QK_NOPE_DIM = 64
QK_ROPE_DIM = 32
V_HEAD_DIM = 64
ATTN_DIM = N_HEADS * V_HEAD_DIM
Q_LORA_RANK = 384
KV_LORA_RANK = 256
ROPE_THETA = 10000.0
N_EXPERTS = 16
CAPACITY_FACTOR = 2
FFN_HIDDEN = 512
LN_EPS = 1e-5
RMS_EPS = 1e-6
DEPTH = 1
DEEPNORM_ALPHA = (2.0 * DEPTH) ** 0.25

LANES = 128
HEAD_PAD = 128
V_ROWS = HEAD_PAD
HEADS_PER_STEP = 4
MOE_TILE = 256
MOE_WIN = 64
ROW_ALIGN = 8
FFN_ROWS = 1024
HALO = 8
VMEM_LIMIT = 56 * 1024 * 1024

BF16 = jnp.bfloat16
F32 = jnp.float32


def _layer_norm(x, g, b):
    mu = jnp.mean(x, axis=-1, keepdims=True)
    xc = x - mu
    var = jnp.mean(xc * xc, axis=-1, keepdims=True)
    return xc * lax.rsqrt(var + LN_EPS) * g + b


def _rms_norm(x, g):
    return x * lax.rsqrt(jnp.mean(x * x, axis=-1, keepdims=True) + RMS_EPS) * g


def _proj_kernel(x_ref, lng_ref, lnb_ref, win_ref, qg_ref, wuq_ref, kvg_ref, wk_ref, wvt_ref,
                 cos_ref, sin_ref, u_ref, q_ref, k_ref, vt_ref):
    xn = _layer_norm(x_ref[0], lng_ref[...], lnb_ref[...])
    proj = jnp.dot(xn.astype(BF16), win_ref[...], preferred_element_type=F32)
    o1 = POOL_DIM
    o2 = o1 + Q_LORA_RANK
    o3 = o2 + KV_LORA_RANK
    u_ref[0] = proj[:, :o1]
    qn = _rms_norm(proj[:, o1:o2], qg_ref[...])
    q = jnp.dot(qn.astype(BF16), wuq_ref[...], preferred_element_type=F32)
    kvn = _rms_norm(proj[:, o2:o3], kvg_ref[...]).astype(BF16)
    kk = jnp.dot(kvn, wk_ref[...], preferred_element_type=F32)
    vvt = lax.dot_general(wvt_ref[...], kvn, (((1,), (1,)), ((), ())),
                          preferred_element_type=F32)
    cos = cos_ref[...]
    sin = sin_ref[...]
    lane = lax.broadcasted_iota(jnp.int32, cos.shape, 1)
    first_half = lane < QK_NOPE_DIM + QK_ROPE_DIM // 2

    def rope(t):
        partner = jnp.where(first_half,
                            pltpu.roll(t, HEAD_PAD - QK_ROPE_DIM // 2, 1),
                            pltpu.roll(t, QK_ROPE_DIM // 2, 1))
        return t * cos + partner * sin

    k_rope = rope(proj[:, o3:o3 + HEAD_PAD])
    row = lax.broadcasted_iota(jnp.int32, (V_ROWS, vvt.shape[1]), 0)
    ones_rows = jnp.where(row >= V_HEAD_DIM, 1.0, 0.0).astype(F32)
    scale = math.log2(math.e) / math.sqrt(QK_NOPE_DIM + QK_ROPE_DIM)
    for h in range(N_HEADS):
        sl = slice(h * HEAD_PAD, (h + 1) * HEAD_PAD)
        q_ref[0, h] = (rope(q[:, sl]) * scale).astype(BF16)
        k_ref[0, h] = (kk[:, sl] + k_rope).astype(BF16)
        vt_ref[0, h] = (vvt[h * V_ROWS:(h + 1) * V_ROWS, :] + ones_rows).astype(BF16)


def _proj_call(x, p, tm):
    b, s, _ = x.shape
    full = lambda shape: pl.BlockSpec(shape, lambda i, j: (0,) * len(shape))
    hs = pl.BlockSpec((1, N_HEADS, tm, HEAD_PAD), lambda i, j: (i, 0, j, 0))
    return pl.pallas_call(
        _proj_kernel,
        grid=(b, s // tm),
        in_specs=[
            pl.BlockSpec((1, tm, D_MODEL), lambda i, j: (i, j, 0)),
            full((1, D_MODEL)), full((1, D_MODEL)),
            full(p["w_in"].shape),
            full((1, Q_LORA_RANK)), full(p["w_uq"].shape),
            full((1, KV_LORA_RANK)), full(p["w_k"].shape), full(p["w_vt"].shape),
            pl.BlockSpec((tm, HEAD_PAD), lambda i, j: (j, 0)),
            pl.BlockSpec((tm, HEAD_PAD), lambda i, j: (j, 0)),
        ],
        out_specs=[
            pl.BlockSpec((1, tm, POOL_DIM), lambda i, j: (i, j, 0)),
            hs, hs, pl.BlockSpec((1, N_HEADS, V_ROWS, tm), lambda i, j: (i, 0, 0, j)),
        ],
        out_shape=[
            jax.ShapeDtypeStruct((b, s, POOL_DIM), F32),
            jax.ShapeDtypeStruct((b, N_HEADS, s, HEAD_PAD), BF16),
            jax.ShapeDtypeStruct((b, N_HEADS, s, HEAD_PAD), BF16),
            jax.ShapeDtypeStruct((b, N_HEADS, V_ROWS, s), BF16),
        ],
        compiler_params=pltpu.CompilerParams(
            dimension_semantics=("arbitrary", "arbitrary"), vmem_limit_bytes=VMEM_LIMIT),
        name="proj",
    )(x, p["ln_in_g"], p["ln_in_b"], p["w_in"], p["q_norm_g"], p["w_uq"], p["kv_norm_g"],
      p["w_k"], p["w_vt"], p["cos"][s], p["sin"][s])


def _attn_kernel(q_ref, k_ref, vt_ref, o_ref, m_ref, acc_ref, s_ref, p_ref, alpha_ref, cmax_ref, *, tk):
    s_len = k_ref.shape[2]
    tq = q_ref.shape[2]
    n_chunks = s_len // tk
    m_ref[...] = jnp.full(m_ref.shape, -jnp.inf, F32)
    acc_ref[...] = jnp.zeros(acc_ref.shape, F32)

    def scores(j, slot):
        start = j * tk
        for hh in range(HEADS_PER_STEP):
            k = k_ref[0, hh, pl.ds(start, tk), :]
            st = lax.dot_general(k, q_ref[0, hh], (((1,), (1,)), ((), ())),
                                 preferred_element_type=F32)
            s_ref[slot, hh] = st
            cmax_ref[slot, hh] = jnp.max(st, axis=0, keepdims=True)

    def softmax(slot):
        for hh in range(HEADS_PER_STEP):
            m_old = m_ref[hh]
            m_new = jnp.maximum(m_old, cmax_ref[slot, hh])
            p_ref[slot, hh] = jnp.exp2(s_ref[slot, hh] - m_new).astype(BF16)
            alpha_ref[slot, hh] = jnp.exp2(m_old - m_new)
            m_ref[hh] = m_new

    def values(j, slot):
        start = j * tk
        for hh in range(HEADS_PER_STEP):
            vt = vt_ref[0, hh, :, pl.ds(start, tk)]
            acc_ref[hh] = acc_ref[hh] * alpha_ref[slot, hh] + jnp.dot(
                vt, p_ref[slot, hh], preferred_element_type=F32)

    scores(0, 0)
    for j in range(n_chunks):
        if j + 1 < n_chunks:
            scores(j + 1, (j + 1) % 2)
        if j >= 1:
            values(j - 1, (j - 1) % 2)
        softmax(j % 2)
    values(n_chunks - 1, (n_chunks - 1) % 2)
    heads = []
    for hh in range(HEADS_PER_STEP):
        acc = acc_ref[hh]
        heads.append((acc / acc[V_HEAD_DIM:V_HEAD_DIM + 1, :]).T)
    lane = lax.broadcasted_iota(jnp.int32, (tq, HEAD_PAD), 1)
    for pair in range(HEADS_PER_STEP // 2):
        out = jnp.where(lane < V_HEAD_DIM, heads[2 * pair], pltpu.roll(heads[2 * pair + 1], V_HEAD_DIM, 1))
        o_ref[0, :, pair * HEAD_PAD:(pair + 1) * HEAD_PAD] = out.astype(BF16)


def _attn_call(q, k, vt, tq, tk):
    b, h, s, _ = q.shape
    hp = HEADS_PER_STEP
    return pl.pallas_call(
        functools.partial(_attn_kernel, tk=tk),
        grid=(b, h // hp, s // tq),
        in_specs=[
            pl.BlockSpec((1, hp, tq, HEAD_PAD), lambda i, j, t: (i, j, t, 0)),
            pl.BlockSpec((1, hp, s, HEAD_PAD), lambda i, j, t: (i, j, 0, 0)),
            pl.BlockSpec((1, hp, V_ROWS, s), lambda i, j, t: (i, j, 0, 0)),
        ],
        out_specs=pl.BlockSpec((1, tq, hp * V_HEAD_DIM), lambda i, j, t: (i, t, j)),
        out_shape=jax.ShapeDtypeStruct((b, s, ATTN_DIM), BF16),
        scratch_shapes=[
            pltpu.VMEM((hp, 1, tq), F32),
            pltpu.VMEM((hp, V_ROWS, tq), F32),
            pltpu.VMEM((2, hp, tk, tq), F32),
            pltpu.VMEM((2, hp, tk, tq), BF16),
            pltpu.VMEM((2, hp, 1, tq), F32),
            pltpu.VMEM((2, hp, 1, tq), F32),
        ],
        compiler_params=pltpu.CompilerParams(
            dimension_semantics=("arbitrary", "arbitrary", "arbitrary"), vmem_limit_bytes=VMEM_LIMIT),
        name="attn",
    )(q, k, vt)


def _mix_kernel(x_ref, up_ref, um_ref, un_ref, at_ref, lng_ref, lnb_ref, wp_ref, ps_ref, wo_ref,
                g1_ref, b1_ref, wr_ref, x1_ref, x1b_ref, aff_ref, ubuf):
    tm = um_ref.shape[1]
    j = pl.program_id(1)
    n_tiles = pl.num_programs(1)
    s_len = n_tiles * tm
    xn = _layer_norm(x_ref[0], lng_ref[...], lnb_ref[...])

    ubuf[pl.ds(0, HALO), :] = jnp.where(j > 0, up_ref[0], 0.0)
    ubuf[pl.ds(HALO, tm), :] = um_ref[0]
    ubuf[pl.ds(HALO + tm, HALO), :] = jnp.where(j < n_tiles - 1, un_ref[0], 0.0)

    edge = lax.broadcasted_iota(jnp.int32, (HALO, 1), 0)

    def inv_count(pos, w):
        lo = jnp.maximum(pos - w // 2, 0)
        hi = jnp.minimum(pos + w // 2 - 1, s_len - 1)
        return 1.0 / (hi - lo + 1).astype(F32)

    mix = jnp.dot(at_ref[0], wo_ref[pl.ds(POOL_DIM, ATTN_DIM), :], preferred_element_type=F32)
    for g, w in enumerate(POOL_WINDOWS):
        cols = pl.ds(g * POOL_GROUP_DIM, POOL_GROUP_DIM)
        n_buf = tm + 2 * HALO
        acc = ubuf[:, cols]
        span = 1
        while span < w // 2:
            acc = acc + pltpu.roll(acc, n_buf - span, 0)
            span *= 2
        acc = acc + pltpu.roll(acc, w // 2, 0)
        win_sum = acc[HALO:HALO + tm]
        u_g = ubuf[pl.ds(HALO, tm), cols]
        d = jnp.concatenate([
            win_sum[:HALO] * inv_count(j * tm + edge, w) - u_g[:HALO],
            win_sum[HALO:tm - HALO] * (1.0 / w) - u_g[HALO:tm - HALO],
            win_sum[tm - HALO:] * inv_count(j * tm + tm - HALO + edge, w) - u_g[tm - HALO:],
        ], axis=0)
        y = jnp.dot(d.astype(BF16), wp_ref[g], preferred_element_type=F32) * ps_ref[:, cols]
        mix = mix + jnp.dot(y.astype(BF16), wo_ref[cols, :], preferred_element_type=F32)

    x1 = _layer_norm(DEEPNORM_ALPHA * xn + mix, g1_ref[...], b1_ref[...])
    x1_ref[0] = x1
    x1_hi = x1.astype(BF16)
    x1b_ref[0] = x1_hi
    x1_lo = (x1 - x1_hi.astype(F32)).astype(BF16)
    logits = (jnp.dot(x1_hi, wr_ref[0], preferred_element_type=F32)
              + jnp.dot(x1_hi, wr_ref[1], preferred_element_type=F32)
              + jnp.dot(x1_lo, wr_ref[0], preferred_element_type=F32))
    e = jnp.exp(logits - jnp.max(logits, axis=-1, keepdims=True))
    aff_ref[0] = e / jnp.sum(e, axis=-1, keepdims=True)


def _mix_call(x, u, attn, p, tm):
    b, s, _ = x.shape
    full = lambda shape: pl.BlockSpec(shape, lambda i, j: (0,) * len(shape))
    tile = lambda width: pl.BlockSpec((1, tm, width), lambda i, j: (i, j, 0))
    hb = tm // HALO
    last_halo_block = s // HALO - 1
    return pl.pallas_call(
        _mix_kernel,
        grid=(b, s // tm),
        in_specs=[
            tile(D_MODEL),
            pl.BlockSpec((1, HALO, POOL_DIM), lambda i, j: (i, jnp.maximum(j * hb - 1, 0), 0)),
            tile(POOL_DIM),
            pl.BlockSpec((1, HALO, POOL_DIM), lambda i, j: (i, jnp.minimum((j + 1) * hb, last_halo_block), 0)),
            tile(ATTN_DIM),
            full((1, D_MODEL)), full((1, D_MODEL)),
            full(p["w_pool"].shape), full((1, POOL_DIM)), full(p["w_o"].shape),
            full((1, D_MODEL)), full((1, D_MODEL)), full(p["w_router"].shape),
        ],
        out_specs=[tile(D_MODEL), tile(D_MODEL), tile(N_EXPERTS)],
        out_shape=[
            jax.ShapeDtypeStruct((b, s, D_MODEL), F32),
            jax.ShapeDtypeStruct((b, s, D_MODEL), BF16),
            jax.ShapeDtypeStruct((b, s, N_EXPERTS), F32),
        ],
        scratch_shapes=[pltpu.VMEM((tm + 2 * HALO, POOL_DIM), F32)],
        compiler_params=pltpu.CompilerParams(
            dimension_semantics=("arbitrary", "arbitrary"), vmem_limit_bytes=VMEM_LIMIT),
        name="mix",
    )(x, u, u, u, attn, p["ln_in_g"], p["ln_in_b"], p["w_pool"], p["pool_scale"], p["w_o"],
      p["ln1_g"], p["ln1_b"], p["w_router"])


def _select_kernel(aff_ref, gate_ref, rows_ref, offs_ref, *, cap, chunk):
    n_exp, t = aff_ref.shape
    n_chunks = t // chunk

    def count(pred):
        def body(c, acc):
            start = pl.multiple_of(c * chunk, chunk)
            bits = lax.bitcast_convert_type(aff_ref[:, pl.ds(start, chunk)], jnp.int32)
            tok = start + lax.broadcasted_iota(jnp.int32, (n_exp, chunk), 1)
            hits = jnp.where(pred(bits, tok), 1.0, 0.0)
            for k in range(chunk // LANES):
                acc = acc + hits[:, k * LANES:(k + 1) * LANES]
            return acc
        partial = lax.fori_loop(0, n_chunks, body, jnp.zeros((n_exp, LANES), F32))
        return jnp.sum(partial, axis=1, keepdims=True)

    def value_step(i, thr):
        cand = thr | (1 << (30 - i))
        n_ge = count(lambda bits, tok: bits >= cand)
        return jnp.where(n_ge >= cap, cand, thr)
    thr = lax.fori_loop(0, 31, value_step, jnp.zeros((n_exp, 1), jnp.int32))

    need = cap - count(lambda bits, tok: bits > thr)
    index_bits = t.bit_length()

    def index_step(i, below):
        cand = below + (1 << (index_bits - 1 - i))
        n_ties = count(lambda bits, tok: (bits == thr) & (tok < cand))
        return jnp.where((cand <= t) & (n_ties < need), cand, below)
    cut = lax.fori_loop(0, index_bits, index_step, jnp.zeros((n_exp, 1), jnp.int32)) + 1

    def write(c, counts):
        start = pl.multiple_of(c * chunk, chunk)
        aff = aff_ref[:, pl.ds(start, chunk)]
        bits = lax.bitcast_convert_type(aff, jnp.int32)
        tok = start + lax.broadcasted_iota(jnp.int32, (n_exp, chunk), 1)
        sel = (bits > thr) | ((bits == thr) & (tok < cut))
        gate = jnp.where(sel, aff, 0.0)
        gate_ref[:, pl.ds(start, chunk)] = gate
        tile_lane = lax.broadcasted_iota(jnp.int32, counts.shape, 1)
        for k in range(chunk // MOE_TILE):
            n = jnp.sum(jnp.where(gate[:, k * MOE_TILE:(k + 1) * MOE_TILE] > 0.0, 1.0, 0.0),
                        axis=1, keepdims=True)
            counts = jnp.where(tile_lane == c * (chunk // MOE_TILE) + k, n, counts)
        return counts
    n_tiles = t // MOE_TILE
    counts = lax.fori_loop(0, n_chunks, write, jnp.zeros((n_exp, n_tiles), F32))

    rows = jnp.floor((counts + (ROW_ALIGN - 1)) * (1.0 / ROW_ALIGN)) * ROW_ALIGN
    before = (lax.broadcasted_iota(jnp.int32, (n_tiles, n_tiles), 0)
              < lax.broadcasted_iota(jnp.int32, (n_tiles, n_tiles), 1))
    offs = jnp.dot(rows.astype(BF16), jnp.where(before, 1.0, 0.0).astype(BF16), preferred_element_type=F32)
    rows_ref[...] = rows.astype(jnp.int32)
    offs_ref[...] = offs.astype(jnp.int32)


def _select_call(aff_t, cap):
    n_exp, t = aff_t.shape
    n_tiles = t // MOE_TILE
    return pl.pallas_call(
        functools.partial(_select_kernel, cap=cap, chunk=2048),
        out_shape=[
            jax.ShapeDtypeStruct((n_exp, t), F32),
            jax.ShapeDtypeStruct((n_exp, n_tiles), jnp.int32),
            jax.ShapeDtypeStruct((n_exp, n_tiles), jnp.int32),
        ],
        compiler_params=pltpu.CompilerParams(vmem_limit_bytes=VMEM_LIMIT),
        name="select",
    )(aff_t)


def _pack_bf16_pair(x):
    half = x.shape[1] // 2
    hi = lax.bitcast_convert_type(x[:, :half], jnp.uint32) & jnp.uint32(0xFFFF0000)
    lo = lax.shift_right_logical(lax.bitcast_convert_type(x[:, half:], jnp.uint32), jnp.uint32(16))
    return hi | lo


def _unpack_bf16_pair(u):
    a = lax.bitcast_convert_type(u & jnp.uint32(0xFFFF0000), F32)
    b = lax.bitcast_convert_type(lax.shift_left(u, jnp.uint32(16)), F32)
    return a.astype(BF16), b.astype(BF16)


def _scalar_max(ref, i):
    m = ref[0, i]
    for e in range(1, N_EXPERTS):
        m = jnp.maximum(m, ref[e, i])
    return m


def _window_copies(ref_of, rows_ref, offs_ref, i, p, buf, sem, to_hbm, only_if_needed):
    copies = []
    for e in range(N_EXPERTS):
        start = pl.multiple_of(offs_ref[e, i] + p * MOE_WIN, ROW_ALIGN)
        hbm = ref_of.at[e, pl.ds(start, MOE_WIN)]
        vmem = buf.at[pl.ds(e * MOE_WIN, MOE_WIN)]
        copy = pltpu.make_async_copy(vmem, hbm, sem) if to_hbm else pltpu.make_async_copy(hbm, vmem, sem)
        needed = rows_ref[e, i] > p * MOE_WIN
        copies.append((copy, needed if only_if_needed else None))
    return copies


def _start_all(copies):
    for copy, cond in copies:
        if cond is None:
            copy.start()
        else:
            pl.when(cond)(copy.start)


def _wait_all(copies):
    for copy, cond in copies:
        if cond is None:
            copy.wait()
        else:
            pl.when(cond)(copy.wait)


def _dispatch_kernel(rows_ref, offs_ref, gt_ref, xb_ref, xe_ref, stack_ref, zero_ref, sem, zero_sem):
    i = pl.program_id(0)
    n_tiles = pl.num_programs(0)
    slot = i % 2
    n_rows = N_EXPERTS * MOE_WIN
    sel_t = jnp.where(gt_ref[...] > 0.0, 1.0, 0.0).astype(BF16)
    earlier = (lax.broadcasted_iota(jnp.int32, (MOE_TILE, MOE_TILE), 0)
               < lax.broadcasted_iota(jnp.int32, (MOE_TILE, MOE_TILE), 1))
    rank_t = jnp.dot(sel_t, jnp.where(earlier, 1.0, 0.0).astype(BF16), preferred_element_type=F32)
    spread = jnp.where(lax.broadcasted_iota(jnp.int32, (n_rows, N_EXPERTS), 0) // MOE_WIN
                       == lax.broadcasted_iota(jnp.int32, (n_rows, N_EXPERTS), 1), 1.0, 0.0).astype(BF16)
    rank_rows = jnp.dot(spread, rank_t.astype(BF16), preferred_element_type=F32)
    sel_rows = jnp.dot(spread, sel_t, preferred_element_type=F32)
    slot_in_win = lax.broadcasted_iota(jnp.int32, (n_rows, 1), 0) % MOE_WIN

    def fill(p):
        want = (slot_in_win + p * MOE_WIN).astype(F32)
        onehot = jnp.where((rank_rows == want) & (sel_rows > 0.0), 1.0, 0.0).astype(BF16)
        picked = jnp.dot(onehot, xb_ref[...], preferred_element_type=F32)
        stack_ref[slot] = _pack_bf16_pair(picked)

    copies = lambda tile, p, s, only_if_needed: _window_copies(
        xe_ref, rows_ref, offs_ref, tile, p, stack_ref.at[s], sem.at[s], True, only_if_needed)
    passes = lambda tile: (_scalar_max(rows_ref, tile) + MOE_WIN - 1) // MOE_WIN

    fill(0)
    prev = jnp.maximum(i - 1, 0)

    @pl.when((i > 0) & (passes(prev) <= 1))
    def _():
        _wait_all(copies(prev, 0, 1 - slot, False))

    _start_all(copies(i, 0, slot, False))
    n_pass = passes(i)

    @pl.when(n_pass > 1)
    def _():
        _wait_all(copies(i, 0, slot, False))

        def extra(p, carry):
            fill(p)
            _start_all(copies(i, p, slot, True))
            _wait_all(copies(i, p, slot, True))
            return carry
        lax.fori_loop(1, n_pass, extra, 0)

    @pl.when((i == n_tiles - 1) & (n_pass <= 1))
    def _():
        _wait_all(copies(i, 0, slot, False))

    @pl.when(i == n_tiles - 1)
    def _():
        cap_rows = xe_ref.shape[1]
        zero_ref[...] = jnp.zeros(zero_ref.shape, jnp.uint32)
        for e in range(N_EXPERTS):
            total = offs_ref[e, i] + rows_ref[e, i]
            n_full = (cap_rows - total) // MOE_WIN

            def zero_copy(start, e=e):
                return pltpu.make_async_copy(
                    zero_ref, xe_ref.at[e, pl.ds(pl.multiple_of(start, ROW_ALIGN), MOE_WIN)], zero_sem)

            def fill_start(k, carry, total=total, zero_copy=zero_copy):
                zero_copy(total + k * MOE_WIN).start()
                return carry

            def fill_wait(k, carry, zero_copy=zero_copy):
                zero_copy(0).wait()
                return carry

            lax.fori_loop(0, n_full, fill_start, 0)
            lax.fori_loop(0, n_full, fill_wait, 0)
            zero_copy(cap_rows - MOE_WIN).start()
            zero_copy(cap_rows - MOE_WIN).wait()


def _dispatch_call(rows, offs, gates_t, x1b, cap_rows):
    n_exp, t = gates_t.shape
    n_tiles = t // MOE_TILE
    grid_spec = pltpu.PrefetchScalarGridSpec(
        num_scalar_prefetch=2,
        grid=(n_tiles,),
        in_specs=[
            pl.BlockSpec((n_exp, MOE_TILE), lambda i, rows, offs: (0, i)),
            pl.BlockSpec((MOE_TILE, D_MODEL), lambda i, rows, offs: (i, 0)),
        ],
        out_specs=pl.BlockSpec(memory_space=pl.ANY),
        scratch_shapes=[
            pltpu.VMEM((2, n_exp * MOE_WIN, D_MODEL // 2), jnp.uint32),
            pltpu.VMEM((MOE_WIN, D_MODEL // 2), jnp.uint32),
            pltpu.SemaphoreType.DMA((2,)),
            pltpu.SemaphoreType.DMA(()),
        ],
    )
    return pl.pallas_call(
        _dispatch_kernel,
        grid_spec=grid_spec,
        out_shape=jax.ShapeDtypeStruct((n_exp, cap_rows, D_MODEL // 2), jnp.uint32),
        compiler_params=pltpu.CompilerParams(
            dimension_semantics=("arbitrary",), vmem_limit_bytes=VMEM_LIMIT),
        name="dispatch",
    )(rows, offs, gates_t, x1b)


def _ffn_kernel(total_ref, xe_ref, wg_ref, wu_ref, wd_ref, ye_ref, wg_bf, wu_bf, wd_bf):
    e = pl.program_id(0)
    blk = pl.program_id(1)
    n_blk_rows = xe_ref.shape[1]
    total = total_ref[e]

    @pl.when(blk == 0)
    def _():
        wg_bf[...] = wg_ref[0].astype(BF16)
        wu_bf[...] = wu_ref[0].astype(BF16)
        wd_bf[...] = wd_ref[0].astype(BF16)

    @pl.when(blk * n_blk_rows < total)
    def _():
        xa, xb = _unpack_bf16_pair(xe_ref[0])
        half = D_MODEL // 2
        hg = (jnp.dot(xa, wg_bf[:half, :], preferred_element_type=F32)
              + jnp.dot(xb, wg_bf[half:, :], preferred_element_type=F32))
        hu = (jnp.dot(xa, wu_bf[:half, :], preferred_element_type=F32)
              + jnp.dot(xb, wu_bf[half:, :], preferred_element_type=F32))
        h = hg * jax.nn.sigmoid(hg) * hu
        ye = jnp.dot(h.astype(BF16), wd_bf[...], preferred_element_type=F32)
        ye_ref[0] = _pack_bf16_pair(ye.astype(BF16).astype(F32))

    @pl.when(blk * n_blk_rows >= total)
    def _():
        ye_ref[...] = jnp.zeros(ye_ref.shape, jnp.uint32)


def _ffn_call(totals, xe, p):
    n_exp, cap_rows, half = xe.shape
    n_blk = cap_rows // FFN_ROWS

    def xe_index(e, b, totals):
        last = jnp.maximum((totals[e] + FFN_ROWS - 1) // FFN_ROWS - 1, 0)
        return (e, jnp.minimum(b, last), 0)

    grid_spec = pltpu.PrefetchScalarGridSpec(
        num_scalar_prefetch=1,
        grid=(n_exp, n_blk),
        in_specs=[
            pl.BlockSpec((1, FFN_ROWS, half), xe_index),
            pl.BlockSpec((1, D_MODEL, FFN_HIDDEN), lambda e, b, totals: (e, 0, 0)),
            pl.BlockSpec((1, D_MODEL, FFN_HIDDEN), lambda e, b, totals: (e, 0, 0)),
            pl.BlockSpec((1, FFN_HIDDEN, D_MODEL), lambda e, b, totals: (e, 0, 0)),
        ],
        out_specs=pl.BlockSpec((1, FFN_ROWS, half), lambda e, b, totals: (e, b, 0)),
        scratch_shapes=[
            pltpu.VMEM((D_MODEL, FFN_HIDDEN), BF16),
            pltpu.VMEM((D_MODEL, FFN_HIDDEN), BF16),
            pltpu.VMEM((FFN_HIDDEN, D_MODEL), BF16),
        ],
    )
    return pl.pallas_call(
        _ffn_kernel,
        grid_spec=grid_spec,
        out_shape=jax.ShapeDtypeStruct(xe.shape, jnp.uint32),
        compiler_params=pltpu.CompilerParams(
            dimension_semantics=("arbitrary", "arbitrary"), vmem_limit_bytes=VMEM_LIMIT),
        name="ffn",
    )(totals, xe, p["w_gate"], p["w_up"], p["w_down"])


def _combine_kernel(rows_ref, offs_ref, g_ref, x1_ref, g2_ref, b2_ref, ye_ref, o_ref, land_ref, moe_ref, sem):
    i = pl.program_id(0)
    n_tiles = pl.num_programs(0)
    slot = i % 2
    n_rows = N_EXPERTS * MOE_WIN
    half = D_MODEL // 2
    fetch = lambda tile, p, s, needed: _window_copies(ye_ref, rows_ref, offs_ref, tile, p, land_ref.at[s],
                                                      sem.at[s], False, needed)

    @pl.when(i == 0)
    def _():
        _start_all(fetch(0, 0, 0, False))

    @pl.when(i + 1 < n_tiles)
    def _():
        _start_all(fetch(jnp.minimum(i + 1, n_tiles - 1), 0, 1 - slot, False))

    gates = g_ref[...]
    sel = jnp.where(gates > 0.0, 1.0, 0.0).astype(BF16)
    earlier = (lax.broadcasted_iota(jnp.int32, (MOE_TILE, MOE_TILE), 1)
               < lax.broadcasted_iota(jnp.int32, (MOE_TILE, MOE_TILE), 0))
    rank = jnp.dot(jnp.where(earlier, 1.0, 0.0).astype(BF16), sel, preferred_element_type=F32)
    spread = jnp.where(lax.broadcasted_iota(jnp.int32, (N_EXPERTS, n_rows), 1) // MOE_WIN
                       == lax.broadcasted_iota(jnp.int32, (N_EXPERTS, n_rows), 0), 1.0, 0.0).astype(BF16)
    rank_cols = jnp.dot(rank.astype(BF16), spread, preferred_element_type=F32)
    gate_cols = jnp.dot(gates.astype(BF16), spread, preferred_element_type=F32)
    slot_in_win = lax.broadcasted_iota(jnp.int32, (1, n_rows), 1) % MOE_WIN
    row_expert = lax.broadcasted_iota(jnp.int32, (n_rows, 1), 0) // MOE_WIN
    row_in_win = lax.broadcasted_iota(jnp.int32, (n_rows, 1), 0) % MOE_WIN
    seg_rows = jnp.zeros((n_rows, 1), jnp.int32)
    for e in range(N_EXPERTS):
        seg_rows = jnp.where(row_expert == e, rows_ref[e, i], seg_rows)

    def gather(p, s):
        want = (slot_in_win + p * MOE_WIN).astype(F32)
        weights = jnp.where(rank_cols == want, gate_cols, 0.0).astype(BF16)
        landed = jnp.where(row_in_win + p * MOE_WIN < seg_rows, land_ref[s], jnp.uint32(0))
        ya, yb = _unpack_bf16_pair(landed)
        return (jnp.dot(weights, ya, preferred_element_type=F32),
                jnp.dot(weights, yb, preferred_element_type=F32))

    _wait_all(fetch(i, 0, slot, False))
    ma, mb = gather(0, slot)
    moe_ref[:, :half] = ma
    moe_ref[:, half:] = mb
    n_pass = (_scalar_max(rows_ref, i) + MOE_WIN - 1) // MOE_WIN

    @pl.when(n_pass > 1)
    def _():
        def extra(p, carry):
            _start_all(fetch(i, p, 2, True))
            _wait_all(fetch(i, p, 2, True))
            ea, eb = gather(p, 2)
            moe_ref[:, :half] += ea
            moe_ref[:, half:] += eb
            return carry
        lax.fori_loop(1, n_pass, extra, 0)

    o_ref[...] = _layer_norm(DEEPNORM_ALPHA * x1_ref[...] + moe_ref[...], g2_ref[...], b2_ref[...])


def _combine_call(rows, offs, gates, x1, ye, p):
    t, n_exp = gates.shape
    n_tiles = t // MOE_TILE
    full = lambda shape: pl.BlockSpec(shape, lambda i, rows, offs: (0,) * len(shape))
    grid_spec = pltpu.PrefetchScalarGridSpec(
        num_scalar_prefetch=2,
        grid=(n_tiles,),
        in_specs=[
            pl.BlockSpec((MOE_TILE, n_exp), lambda i, rows, offs: (i, 0)),
            pl.BlockSpec((MOE_TILE, D_MODEL), lambda i, rows, offs: (i, 0)),
            full((1, D_MODEL)), full((1, D_MODEL)),
            pl.BlockSpec(memory_space=pl.ANY),
        ],
        out_specs=pl.BlockSpec((MOE_TILE, D_MODEL), lambda i, rows, offs: (i, 0)),
        scratch_shapes=[
            pltpu.VMEM((3, n_exp * MOE_WIN, D_MODEL // 2), jnp.uint32),
            pltpu.VMEM((MOE_TILE, D_MODEL), F32),
            pltpu.SemaphoreType.DMA((3,)),
        ],
    )
    return pl.pallas_call(
        _combine_kernel,
        grid_spec=grid_spec,
        out_shape=jax.ShapeDtypeStruct((t, D_MODEL), F32),
        compiler_params=pltpu.CompilerParams(
            dimension_semantics=("arbitrary",), vmem_limit_bytes=VMEM_LIMIT),
        name="combine",
    )(rows, offs, gates, x1, p["ln2_g"], p["ln2_b"], ye)


def _rope_tables(s):
    half = QK_ROPE_DIM // 2
    inv = 1.0 / (ROPE_THETA ** (jnp.arange(0, QK_ROPE_DIM, 2, dtype=F32) / QK_ROPE_DIM))
    ang = jnp.arange(s, dtype=F32)[:, None] * inv[None, :]
    cos, sin = jnp.cos(ang), jnp.sin(ang)
    tail = HEAD_PAD - QK_NOPE_DIM - QK_ROPE_DIM
    cos_t = jnp.concatenate([jnp.ones((s, QK_NOPE_DIM), F32), cos, cos, jnp.ones((s, tail), F32)], axis=1)
    sin_t = jnp.concatenate([jnp.zeros((s, QK_NOPE_DIM), F32), -sin, sin, jnp.zeros((s, tail), F32)], axis=1)
    assert half * 2 == QK_ROPE_DIM
    return cos_t, sin_t


def _prepare_params(seq_lens, ln_in_g, ln_in_b, w_in, w_pool, pool_scale, q_norm_g, w_uq, kv_norm_g, w_ukv,
                    w_o, ln1_g, ln1_b, w_router, w_gate, w_up, w_down, ln2_g, ln2_b):
    row = lambda a: a.reshape(1, -1).astype(F32)
    o3 = POOL_DIM + Q_LORA_RANK + KV_LORA_RANK
    w_in0 = w_in[0]
    w_in_p = jnp.concatenate([
        w_in0[:, :o3], jnp.zeros((D_MODEL, QK_NOPE_DIM), F32), w_in0[:, o3:],
        jnp.zeros((D_MODEL, HEAD_PAD - QK_NOPE_DIM - QK_ROPE_DIM), F32)], axis=1)
    qk = QK_NOPE_DIM + QK_ROPE_DIM
    w_uq_p = jnp.pad(w_uq[0].reshape(Q_LORA_RANK, N_HEADS, qk), ((0, 0), (0, 0), (0, HEAD_PAD - qk)))
    w_ukv0 = w_ukv[0].reshape(KV_LORA_RANK, N_HEADS, QK_NOPE_DIM + V_HEAD_DIM)
    w_k = jnp.pad(w_ukv0[:, :, :QK_NOPE_DIM], ((0, 0), (0, 0), (0, HEAD_PAD - QK_NOPE_DIM)))
    w_v = jnp.pad(w_ukv0[:, :, QK_NOPE_DIM:], ((0, 0), (0, 0), (0, V_ROWS - V_HEAD_DIM)))
    tables = {s: _rope_tables(s) for s in seq_lens}
    return {
        "ln_in_g": row(ln_in_g), "ln_in_b": row(ln_in_b),
        "w_in": w_in_p.astype(BF16),
        "w_pool": w_pool[0].astype(BF16), "pool_scale": row(pool_scale[0]),
        "q_norm_g": row(q_norm_g[0]), "w_uq": w_uq_p.reshape(Q_LORA_RANK, N_HEADS * HEAD_PAD).astype(BF16),
        "kv_norm_g": row(kv_norm_g[0]),
        "w_k": w_k.reshape(KV_LORA_RANK, N_HEADS * HEAD_PAD).astype(BF16),
        "w_vt": w_v.reshape(KV_LORA_RANK, N_HEADS * V_ROWS).T.astype(BF16),
        "w_o": w_o[0].astype(BF16),
        "ln1_g": row(ln1_g[0]), "ln1_b": row(ln1_b[0]),
        "w_router": jnp.stack([w_router[0].astype(BF16),
                               (w_router[0] - w_router[0].astype(BF16).astype(F32)).astype(BF16)]),
        "w_gate": w_gate[0], "w_up": w_up[0], "w_down": w_down[0],
        "ln2_g": row(ln2_g[0]), "ln2_b": row(ln2_b[0]),
        "cos": {s: t[0] for s, t in tables.items()},
        "sin": {s: t[1] for s, t in tables.items()},
    }


def _trunk(x, p):
    b, s, d = x.shape
    t = b * s
    u, q, k, vt = _proj_call(x, p, tm=1024)
    attn = _attn_call(q, k, vt, tq=512, tk=512)
    x1, x1b, aff = _mix_call(x, u, attn, p, tm=1024)
    cap = max(1, (CAPACITY_FACTOR * t) // N_EXPERTS)
    gates_t, rows, offs = _select_call(aff.reshape(t, N_EXPERTS).T, cap)
    cap_rows = -(-(cap + ROW_ALIGN * (t // MOE_TILE) + MOE_WIN) // FFN_ROWS) * FFN_ROWS
    xe = _dispatch_call(rows, offs, gates_t, x1b.reshape(t, d), cap_rows)
    totals = offs[:, -1] + rows[:, -1]
    ye = _ffn_call(totals, xe, p)
    y = _combine_call(rows, offs, gates_t.T, x1.reshape(t, d), ye, p)
    return y.reshape(b, s, d)


def kernel(x_prompt, x_sample, ln_in_g, ln_in_b, w_in, w_pool, pool_scale, q_norm_g, w_uq, kv_norm_g, w_ukv,
           w_o, ln1_g, ln1_b, w_router, w_gate, w_up, w_down, ln2_g, ln2_b):
    assert w_in.shape[0] == DEPTH == 1
    p = _prepare_params((x_prompt.shape[1], x_sample.shape[1]), ln_in_g, ln_in_b, w_in, w_pool, pool_scale,
                        q_norm_g, w_uq, kv_norm_g, w_ukv, w_o, ln1_g, ln1_b, w_router, w_gate, w_up, w_down,
                        ln2_g, ln2_b)
    return (_trunk(x_prompt, p), _trunk(x_sample, p))
```

```python
import functools
import math

import jax
import jax.numpy as jnp
from jax import lax
from jax.experimental import pallas as pl
from jax.experimental.pallas import tpu as pltpu

D_MODEL = 1024
POOL_WINDOWS = (2, 4, 8, 16)
POOL_DIM = 512
POOL_GROUP_DIM = 128
N_HEADS = 8
QK_NOPE_DIM = 64
QK_ROPE_DIM = 32
V_HEAD_DIM = 64
ATTN_DIM = N_HEADS * V_HEAD_DIM
Q_LORA_RANK = 384
KV_LORA_RANK = 256
ROPE_THETA = 10000.0
N_EXPERTS = 16
CAPACITY_FACTOR = 2
FFN_HIDDEN = 512
LN_EPS = 1e-5
RMS_EPS = 1e-6
DEPTH = 1
DEEPNORM_ALPHA = (2.0 * DEPTH) ** 0.25

LANES = 128
HEAD_PAD = 128
V_ROWS = HEAD_PAD
HEADS_PER_STEP = 4
MOE_TILE = 256
MOE_WIN = 64
ROW_ALIGN = 8
FFN_ROWS = 1024
HALO = 8
VMEM_LIMIT = 56 * 1024 * 1024

BF16 = jnp.bfloat16
F32 = jnp.float32


def _layer_norm(x, g, b):
    mu = jnp.mean(x, axis=-1, keepdims=True)
    xc = x - mu
    var = jnp.mean(xc * xc, axis=-1, keepdims=True)
    return xc * lax.rsqrt(var + LN_EPS) * g + b


def _rms_norm(x, g):
    return x * lax.rsqrt(jnp.mean(x * x, axis=-1, keepdims=True) + RMS_EPS) * g


def _proj_kernel(x_ref, lng_ref, lnb_ref, win_ref, qg_ref, wuq_ref, kvg_ref, wk_ref, wvt_ref,
                 cos_ref, sin_ref, u_ref, q_ref, k_ref, vt_ref):
    xn = _layer_norm(x_ref[0], lng_ref[...], lnb_ref[...])
    proj = jnp.dot(xn.astype(BF16), win_ref[...], preferred_element_type=F32)
    o1 = POOL_DIM
    o2 = o1 + Q_LORA_RANK
    o3 = o2 + KV_LORA_RANK
    u_ref[0] = proj[:, :o1]
    qn = _rms_norm(proj[:, o1:o2], qg_ref[...])
    q = jnp.dot(qn.astype(BF16), wuq_ref[...], preferred_element_type=F32)
    kvn = _rms_norm(proj[:, o2:o3], kvg_ref[...]).astype(BF16)
    kk = jnp.dot(kvn, wk_ref[...], preferred_element_type=F32)
    vvt = lax.dot_general(wvt_ref[...], kvn, (((1,), (1,)), ((), ())),
                          preferred_element_type=F32)
    cos = cos_ref[...]
    sin = sin_ref[...]
    lane = lax.broadcasted_iota(jnp.int32, cos.shape, 1)
    first_half = lane < QK_NOPE_DIM + QK_ROPE_DIM // 2

    def rope(t):
        partner = jnp.where(first_half,
                            pltpu.roll(t, HEAD_PAD - QK_ROPE_DIM // 2, 1),
                            pltpu.roll(t, QK_ROPE_DIM // 2, 1))
        return t * cos + partner * sin

    k_rope = rope(proj[:, o3:o3 + HEAD_PAD])
    row = lax.broadcasted_iota(jnp.int32, (V_ROWS, vvt.shape[1]), 0)
    ones_rows = jnp.where(row >= V_HEAD_DIM, 1.0, 0.0).astype(F32)
    scale = math.log2(math.e) / math.sqrt(QK_NOPE_DIM + QK_ROPE_DIM)
    for h in range(N_HEADS):
        sl = slice(h * HEAD_PAD, (h + 1) * HEAD_PAD)
        q_ref[0, h] = (rope(q[:, sl]) * scale).astype(BF16)
        k_ref[0, h] = (kk[:, sl] + k_rope).astype(BF16)
        vt_ref[0, h] = (vvt[h * V_ROWS:(h + 1) * V_ROWS, :] + ones_rows).astype(BF16)


def _proj_call(x, p, tm):
    b, s, _ = x.shape
    full = lambda shape: pl.BlockSpec(shape, lambda i, j: (0,) * len(shape))
    hs = pl.BlockSpec((1, N_HEADS, tm, HEAD_PAD), lambda i, j: (i, 0, j, 0))
    return pl.pallas_call(
        _proj_kernel,
        grid=(b, s // tm),
        in_specs=[
            pl.BlockSpec((1, tm, D_MODEL), lambda i, j: (i, j, 0)),
            full((1, D_MODEL)), full((1, D_MODEL)),
            full(p["w_in"].shape),
            full((1, Q_LORA_RANK)), full(p["w_uq"].shape),
            full((1, KV_LORA_RANK)), full(p["w_k"].shape), full(p["w_vt"].shape),
            pl.BlockSpec((tm, HEAD_PAD), lambda i, j: (j, 0)),
            pl.BlockSpec((tm, HEAD_PAD), lambda i, j: (j, 0)),
        ],
        out_specs=[
            pl.BlockSpec((1, tm, POOL_DIM), lambda i, j: (i, j, 0)),
            hs, hs, pl.BlockSpec((1, N_HEADS, V_ROWS, tm), lambda i, j: (i, 0, 0, j)),
        ],
        out_shape=[
            jax.ShapeDtypeStruct((b, s, POOL_DIM), F32),
            jax.ShapeDtypeStruct((b, N_HEADS, s, HEAD_PAD), BF16),
            jax.ShapeDtypeStruct((b, N_HEADS, s, HEAD_PAD), BF16),
            jax.ShapeDtypeStruct((b, N_HEADS, V_ROWS, s), BF16),
        ],
        compiler_params=pltpu.CompilerParams(
            dimension_semantics=("arbitrary", "arbitrary"), vmem_limit_bytes=VMEM_LIMIT),
        name="proj",
    )(x, p["ln_in_g"], p["ln_in_b"], p["w_in"], p["q_norm_g"], p["w_uq"], p["kv_norm_g"],
      p["w_k"], p["w_vt"], p["cos"][s], p["sin"][s])


def _attn_kernel(q_ref, k_ref, vt_ref, o_ref, m_ref, acc_ref, s_ref, p_ref, alpha_ref, cmax_ref, *, tk):
    s_len = k_ref.shape[2]
    tq = q_ref.shape[2]
    n_chunks = s_len // tk
    m_ref[...] = jnp.full(m_ref.shape, -jnp.inf, F32)
    acc_ref[...] = jnp.zeros(acc_ref.shape, F32)

    def scores(j, slot):
        start = j * tk
        for hh in range(HEADS_PER_STEP):
            k = k_ref[0, hh, pl.ds(start, tk), :]
            st = lax.dot_general(k, q_ref[0, hh], (((1,), (1,)), ((), ())),
                                 preferred_element_type=F32)
            s_ref[slot, hh] = st
            cmax_ref[slot, hh] = jnp.max(st, axis=0, keepdims=True)

    def softmax(slot):
        for hh in range(HEADS_PER_STEP):
            m_old = m_ref[hh]
            m_new = jnp.maximum(m_old, cmax_ref[slot, hh])
            p_ref[slot, hh] = jnp.exp2(s_ref[slot, hh] - m_new).astype(BF16)
            alpha_ref[slot, hh] = jnp.exp2(m_old - m_new)
            m_ref[hh] = m_new

    def values(j, slot):
        start = j * tk
        for hh in range(HEADS_PER_STEP):
            vt = vt_ref[0, hh, :, pl.ds(start, tk)]
            acc_ref[hh] = acc_ref[hh] * alpha_ref[slot, hh] + jnp.dot(
                vt, p_ref[slot, hh], preferred_element_type=F32)

    scores(0, 0)
    for j in range(n_chunks):
        if j + 1 < n_chunks:
            scores(j + 1, (j + 1) % 2)
        if j >= 1:
            values(j - 1, (j - 1) % 2)
        softmax(j % 2)
    values(n_chunks - 1, (n_chunks - 1) % 2)
    heads = []
    for hh in range(HEADS_PER_STEP):
        acc = acc_ref[hh]
        heads.append((acc / acc[V_HEAD_DIM:V_HEAD_DIM + 1, :]).T)
    lane = lax.broadcasted_iota(jnp.int32, (tq, HEAD_PAD), 1)
    for pair in range(HEADS_PER_STEP // 2):
        out = jnp.where(lane < V_HEAD_DIM, heads[2 * pair], pltpu.roll(heads[2 * pair + 1], V_HEAD_DIM, 1))
        o_ref[0, :, pair * HEAD_PAD:(pair + 1) * HEAD_PAD] = out.astype(BF16)


def _attn_call(q, k, vt, tq, tk):
    b, h, s, _ = q.shape
    hp = HEADS_PER_STEP
    return pl.pallas_call(
        functools.partial(_attn_kernel, tk=tk),
        grid=(b, h // hp, s // tq),
        in_specs=[
            pl.BlockSpec((1, hp, tq, HEAD_PAD), lambda i, j, t: (i, j, t, 0)),
            pl.BlockSpec((1, hp, s, HEAD_PAD), lambda i, j, t: (i, j, 0, 0)),
            pl.BlockSpec((1, hp, V_ROWS, s), lambda i, j, t: (i, j, 0, 0)),
        ],
        out_specs=pl.BlockSpec((1, tq, hp * V_HEAD_DIM), lambda i, j, t: (i, t, j)),
        out_shape=jax.ShapeDtypeStruct((b, s, ATTN_DIM), BF16),
        scratch_shapes=[
            pltpu.VMEM((hp, 1, tq), F32),
            pltpu.VMEM((hp, V_ROWS, tq), F32),
            pltpu.VMEM((2, hp, tk, tq), F32),
            pltpu.VMEM((2, hp, tk, tq), BF16),
            pltpu.VMEM((2, hp, 1, tq), F32),
            pltpu.VMEM((2, hp, 1, tq), F32),
        ],
        compiler_params=pltpu.CompilerParams(
            dimension_semantics=("arbitrary", "arbitrary", "arbitrary"), vmem_limit_bytes=VMEM_LIMIT),
        name="attn",
    )(q, k, vt)


def _mix_kernel(x_ref, up_ref, um_ref, un_ref, at_ref, lng_ref, lnb_ref, wp_ref, ps_ref, wo_ref,
                g1_ref, b1_ref, wr_ref, x1_ref, x1b_ref, aff_ref, ubuf):
    tm = um_ref.shape[1]
    j = pl.program_id(1)
    n_tiles = pl.num_programs(1)
    s_len = n_tiles * tm
    xn = _layer_norm(x_ref[0], lng_ref[...], lnb_ref[...])

    ubuf[pl.ds(0, HALO), :] = jnp.where(j > 0, up_ref[0], 0.0)
    ubuf[pl.ds(HALO, tm), :] = um_ref[0]
    ubuf[pl.ds(HALO + tm, HALO), :] = jnp.where(j < n_tiles - 1, un_ref[0], 0.0)

    edge = lax.broadcasted_iota(jnp.int32, (HALO, 1), 0)

    def inv_count(pos, w):
        lo = jnp.maximum(pos - w // 2, 0)
        hi = jnp.minimum(pos + w // 2 - 1, s_len - 1)
        return 1.0 / (hi - lo + 1).astype(F32)

    mix = jnp.dot(at_ref[0], wo_ref[pl.ds(POOL_DIM, ATTN_DIM), :], preferred_element_type=F32)
    for g, w in enumerate(POOL_WINDOWS):
        cols = pl.ds(g * POOL_GROUP_DIM, POOL_GROUP_DIM)
        n_buf = tm + 2 * HALO
        acc = ubuf[:, cols]
        span = 1
        while span < w // 2:
            acc = acc + pltpu.roll(acc, n_buf - span, 0)
            span *= 2
        acc = acc + pltpu.roll(acc, w // 2, 0)
        win_sum = acc[HALO:HALO + tm]
        u_g = ubuf[pl.ds(HALO, tm), cols]
        d = jnp.concatenate([
            win_sum[:HALO] * inv_count(j * tm + edge, w) - u_g[:HALO],
            win_sum[HALO:tm - HALO] * (1.0 / w) - u_g[HALO:tm - HALO],
            win_sum[tm - HALO:] * inv_count(j * tm + tm - HALO + edge, w) - u_g[tm - HALO:],
        ], axis=0)
        y = jnp.dot(d.astype(BF16), wp_ref[g], preferred_element_type=F32) * ps_ref[:, cols]
        mix = mix + jnp.dot(y.astype(BF16), wo_ref[cols, :], preferred_element_type=F32)

    x1 = _layer_norm(DEEPNORM_ALPHA * xn + mix, g1_ref[...], b1_ref[...])
    x1_ref[0] = x1
    x1_hi = x1.astype(BF16)
    x1b_ref[0] = x1_hi
    x1_lo = (x1 - x1_hi.astype(F32)).astype(BF16)
    logits = (jnp.dot(x1_hi, wr_ref[0], preferred_element_type=F32)
              + jnp.dot(x1_hi, wr_ref[1], preferred_element_type=F32)
              + jnp.dot(x1_lo, wr_ref[0], preferred_element_type=F32))
    e = jnp.exp(logits - jnp.max(logits, axis=-1, keepdims=True))
    aff_ref[0] = e / jnp.sum(e, axis=-1, keepdims=True)


def _mix_call(x, u, attn, p, tm):
    b, s, _ = x.shape
    full = lambda shape: pl.BlockSpec(shape, lambda i, j: (0,) * len(shape))
    tile = lambda width: pl.BlockSpec((1, tm, width), lambda i, j: (i, j, 0))
    hb = tm // HALO
    last_halo_block = s // HALO - 1
    return pl.pallas_call(
        _mix_kernel,
        grid=(b, s // tm),
        in_specs=[
            tile(D_MODEL),
            pl.BlockSpec((1, HALO, POOL_DIM), lambda i, j: (i, jnp.maximum(j * hb - 1, 0), 0)),
            tile(POOL_DIM),
            pl.BlockSpec((1, HALO, POOL_DIM), lambda i, j: (i, jnp.minimum((j + 1) * hb, last_halo_block), 0)),
            tile(ATTN_DIM),
            full((1, D_MODEL)), full((1, D_MODEL)),
            full(p["w_pool"].shape), full((1, POOL_DIM)), full(p["w_o"].shape),
            full((1, D_MODEL)), full((1, D_MODEL)), full(p["w_router"].shape),
        ],
        out_specs=[tile(D_MODEL), tile(D_MODEL), tile(N_EXPERTS)],
        out_shape=[
            jax.ShapeDtypeStruct((b, s, D_MODEL), F32),
            jax.ShapeDtypeStruct((b, s, D_MODEL), BF16),
            jax.ShapeDtypeStruct((b, s, N_EXPERTS), F32),
        ],
        scratch_shapes=[pltpu.VMEM((tm + 2 * HALO, POOL_DIM), F32)],
        compiler_params=pltpu.CompilerParams(
            dimension_semantics=("arbitrary", "arbitrary"), vmem_limit_bytes=VMEM_LIMIT),
        name="mix",
    )(x, u, u, u, attn, p["ln_in_g"], p["ln_in_b"], p["w_pool"], p["pool_scale"], p["w_o"],
      p["ln1_g"], p["ln1_b"], p["w_router"])


def _select_kernel(aff_ref, gate_ref, rows_ref, offs_ref, *, cap, chunk):
    n_exp, t = aff_ref.shape
    n_chunks = t // chunk

    def count(pred):
        def body(c, acc):
            start = pl.multiple_of(c * chunk, chunk)
            bits = lax.bitcast_convert_type(aff_ref[:, pl.ds(start, chunk)], jnp.int32)
            tok = start + lax.broadcasted_iota(jnp.int32, (n_exp, chunk), 1)
            hits = jnp.where(pred(bits, tok), 1.0, 0.0)
            for k in range(chunk // LANES):
                acc = acc + hits[:, k * LANES:(k + 1) * LANES]
            return acc
        partial = lax.fori_loop(0, n_chunks, body, jnp.zeros((n_exp, LANES), F32))
        return jnp.sum(partial, axis=1, keepdims=True)

    def value_step(i, thr):
        cand = thr | (1 << (30 - i))
        n_ge = count(lambda bits, tok: bits >= cand)
        return jnp.where(n_ge >= cap, cand, thr)
    thr = lax.fori_loop(0, 31, value_step, jnp.zeros((n_exp, 1), jnp.int32))

    need = cap - count(lambda bits, tok: bits > thr)
    index_bits = t.bit_length()

    def index_step(i, below):
        cand = below + (1 << (index_bits - 1 - i))
        n_ties = count(lambda bits, tok: (bits == thr) & (tok < cand))
        return jnp.where((cand <= t) & (n_ties < need), cand, below)
    cut = lax.fori_loop(0, index_bits, index_step, jnp.zeros((n_exp, 1), jnp.int32)) + 1

    def write(c, counts):
        start = pl.multiple_of(c * chunk, chunk)
        aff = aff_ref[:, pl.ds(start, chunk)]
        bits = lax.bitcast_convert_type(aff, jnp.int32)
        tok = start + lax.broadcasted_iota(jnp.int32, (n_exp, chunk), 1)
        sel = (bits > thr) | ((bits == thr) & (tok < cut))
        gate = jnp.where(sel, aff, 0.0)
        gate_ref[:, pl.ds(start, chunk)] = gate
        tile_lane = lax.broadcasted_iota(jnp.int32, counts.shape, 1)
        for k in range(chunk // MOE_TILE):
            n = jnp.sum(jnp.where(gate[:, k * MOE_TILE:(k + 1) * MOE_TILE] > 0.0, 1.0, 0.0),
                        axis=1, keepdims=True)
            counts = jnp.where(tile_lane == c * (chunk // MOE_TILE) + k, n, counts)
        return counts
    n_tiles = t // MOE_TILE
    counts = lax.fori_loop(0, n_chunks, write, jnp.zeros((n_exp, n_tiles), F32))

    rows = jnp.floor((counts + (ROW_ALIGN - 1)) * (1.0 / ROW_ALIGN)) * ROW_ALIGN
    before = (lax.broadcasted_iota(jnp.int32, (n_tiles, n_tiles), 0)
              < lax.broadcasted_iota(jnp.int32, (n_tiles, n_tiles), 1))
    offs = jnp.dot(rows.astype(BF16), jnp.where(before, 1.0, 0.0).astype(BF16), preferred_element_type=F32)
    rows_ref[...] = rows.astype(jnp.int32)
    offs_ref[...] = offs.astype(jnp.int32)


def _select_call(aff_t, cap):
    n_exp, t = aff_t.shape
    n_tiles = t // MOE_TILE
    return pl.pallas_call(
        functools.partial(_select_kernel, cap=cap, chunk=2048),
        out_shape=[
            jax.ShapeDtypeStruct((n_exp, t), F32),
            jax.ShapeDtypeStruct((n_exp, n_tiles), jnp.int32),
            jax.ShapeDtypeStruct((n_exp, n_tiles), jnp.int32),
        ],
        compiler_params=pltpu.CompilerParams(vmem_limit_bytes=VMEM_LIMIT),
        name="select",
    )(aff_t)


def _pack_bf16_pair(x):
    half = x.shape[1] // 2
    hi = lax.bitcast_convert_type(x[:, :half], jnp.uint32) & jnp.uint32(0xFFFF0000)
    lo = lax.shift_right_logical(lax.bitcast_convert_type(x[:, half:], jnp.uint32), jnp.uint32(16))
    return hi | lo


def _unpack_bf16_pair(u):
    a = lax.bitcast_convert_type(u & jnp.uint32(0xFFFF0000), F32)
    b = lax.bitcast_convert_type(lax.shift_left(u, jnp.uint32(16)), F32)
    return a.astype(BF16), b.astype(BF16)


def _scalar_max(ref, i):
    m = ref[0, i]
    for e in range(1, N_EXPERTS):
        m = jnp.maximum(m, ref[e, i])
    return m


def _window_copies(ref_of, rows_ref, offs_ref, i, p, buf, sem, to_hbm, only_if_needed):
    copies = []
    for e in range(N_EXPERTS):
        start = pl.multiple_of(offs_ref[e, i] + p * MOE_WIN, ROW_ALIGN)
        hbm = ref_of.at[e, pl.ds(start, MOE_WIN)]
        vmem = buf.at[pl.ds(e * MOE_WIN, MOE_WIN)]
        copy = pltpu.make_async_copy(vmem, hbm, sem) if to_hbm else pltpu.make_async_copy(hbm, vmem, sem)
        needed = rows_ref[e, i] > p * MOE_WIN
        copies.append((copy, needed if only_if_needed else None))
    return copies


def _start_all(copies):
    for n, (copy, cond) in enumerate(copies):
        start = functools.partial(copy.start, priority=n % 2)
        if cond is None:
            start()
        else:
            pl.when(cond)(start)


def _wait_all(copies):
    for copy, cond in copies:
        if cond is None:
            copy.wait()
        else:
            pl.when(cond)(copy.wait)


def _dispatch_kernel(rows_ref, offs_ref, gt_ref, xb_ref, xe_ref, stack_ref, zero_ref, sem, zero_sem):
    i = pl.program_id(0)
    n_tiles = pl.num_programs(0)
    slot = i % 2
    n_rows = N_EXPERTS * MOE_WIN
    sel_t = jnp.where(gt_ref[...] > 0.0, 1.0, 0.0).astype(BF16)
    earlier = (lax.broadcasted_iota(jnp.int32, (MOE_TILE, MOE_TILE), 0)
               < lax.broadcasted_iota(jnp.int32, (MOE_TILE, MOE_TILE), 1))
    rank_t = jnp.dot(sel_t, jnp.where(earlier, 1.0, 0.0).astype(BF16), preferred_element_type=F32)
    spread = jnp.where(lax.broadcasted_iota(jnp.int32, (n_rows, N_EXPERTS), 0) // MOE_WIN
                       == lax.broadcasted_iota(jnp.int32, (n_rows, N_EXPERTS), 1), 1.0, 0.0).astype(BF16)
    rank_rows = jnp.dot(spread, rank_t.astype(BF16), preferred_element_type=F32)
    sel_rows = jnp.dot(spread, sel_t, preferred_element_type=F32)
    slot_in_win = lax.broadcasted_iota(jnp.int32, (n_rows, 1), 0) % MOE_WIN

    def fill(p):
        want = (slot_in_win + p * MOE_WIN).astype(F32)
        onehot = jnp.where((rank_rows == want) & (sel_rows > 0.0), 1.0, 0.0).astype(BF16)
        picked = jnp.dot(onehot, xb_ref[...], preferred_element_type=F32)
        stack_ref[slot] = _pack_bf16_pair(picked)

    copies = lambda tile, p, s, only_if_needed: _window_copies(
        xe_ref, rows_ref, offs_ref, tile, p, stack_ref.at[s], sem.at[s], True, only_if_needed)
    passes = lambda tile: (_scalar_max(rows_ref, tile) + MOE_WIN - 1) // MOE_WIN

    fill(0)
    prev = jnp.maximum(i - 1, 0)

    @pl.when((i > 0) & (passes(prev) <= 1))
    def _():
        _wait_all(copies(prev, 0, 1 - slot, False))

    _start_all(copies(i, 0, slot, False))
    n_pass = passes(i)

    @pl.when(n_pass > 1)
    def _():
        _wait_all(copies(i, 0, slot, False))

        def extra(p, carry):
            fill(p)
            _start_all(copies(i, p, slot, True))
            _wait_all(copies(i, p, slot, True))
            return carry
        lax.fori_loop(1, n_pass, extra, 0)

    @pl.when((i == n_tiles - 1) & (n_pass <= 1))
    def _():
        _wait_all(copies(i, 0, slot, False))

    @pl.when(i == n_tiles - 1)
    def _():
        cap_rows = xe_ref.shape[1]
        zero_ref[...] = jnp.zeros(zero_ref.shape, jnp.uint32)
        for e in range(N_EXPERTS):
            total = offs_ref[e, i] + rows_ref[e, i]
            n_full = (cap_rows - total) // MOE_WIN

            def zero_copy(start, e=e):
                return pltpu.make_async_copy(
                    zero_ref, xe_ref.at[e, pl.ds(pl.multiple_of(start, ROW_ALIGN), MOE_WIN)], zero_sem)

            def fill_start(k, carry, total=total, zero_copy=zero_copy):
                zero_copy(total + k * MOE_WIN).start()
                return carry

            def fill_wait(k, carry, zero_copy=zero_copy):
                zero_copy(0).wait()
                return carry

            lax.fori_loop(0, n_full, fill_start, 0)
            lax.fori_loop(0, n_full, fill_wait, 0)
            zero_copy(cap_rows - MOE_WIN).start()
            zero_copy(cap_rows - MOE_WIN).wait()


def _dispatch_call(rows, offs, gates_t, x1b, cap_rows):
    n_exp, t = gates_t.shape
    n_tiles = t // MOE_TILE
    grid_spec = pltpu.PrefetchScalarGridSpec(
        num_scalar_prefetch=2,
        grid=(n_tiles,),
        in_specs=[
            pl.BlockSpec((n_exp, MOE_TILE), lambda i, rows, offs: (0, i)),
            pl.BlockSpec((MOE_TILE, D_MODEL), lambda i, rows, offs: (i, 0)),
        ],
        out_specs=pl.BlockSpec(memory_space=pl.ANY),
        scratch_shapes=[
            pltpu.VMEM((2, n_exp * MOE_WIN, D_MODEL // 2), jnp.uint32),
            pltpu.VMEM((MOE_WIN, D_MODEL // 2), jnp.uint32),
            pltpu.SemaphoreType.DMA((2,)),
            pltpu.SemaphoreType.DMA(()),
        ],
    )
    return pl.pallas_call(
        _dispatch_kernel,
        grid_spec=grid_spec,
        out_shape=jax.ShapeDtypeStruct((n_exp, cap_rows, D_MODEL // 2), jnp.uint32),
        compiler_params=pltpu.CompilerParams(
            dimension_semantics=("arbitrary",), vmem_limit_bytes=VMEM_LIMIT),
        name="dispatch",
    )(rows, offs, gates_t, x1b)


def _ffn_kernel(total_ref, xe_ref, wg_ref, wu_ref, wd_ref, ye_ref, wg_bf, wu_bf, wd_bf):
    e = pl.program_id(0)
    blk = pl.program_id(1)
    n_blk_rows = xe_ref.shape[1]
    total = total_ref[e]

    @pl.when(blk == 0)
    def _():
        wg_bf[...] = wg_ref[0].astype(BF16)
        wu_bf[...] = wu_ref[0].astype(BF16)
        wd_bf[...] = wd_ref[0].astype(BF16)

    @pl.when(blk * n_blk_rows < total)
    def _():
        xa, xb = _unpack_bf16_pair(xe_ref[0])
        half = D_MODEL // 2
        hg = (jnp.dot(xa, wg_bf[:half, :], preferred_element_type=F32)
              + jnp.dot(xb, wg_bf[half:, :], preferred_element_type=F32))
        hu = (jnp.dot(xa, wu_bf[:half, :], preferred_element_type=F32)
              + jnp.dot(xb, wu_bf[half:, :], preferred_element_type=F32))
        h = hg * jax.nn.sigmoid(hg) * hu
        ye = jnp.dot(h.astype(BF16), wd_bf[...], preferred_element_type=F32)
        ye_ref[0] = _pack_bf16_pair(ye.astype(BF16).astype(F32))

    @pl.when(blk * n_blk_rows >= total)
    def _():
        ye_ref[...] = jnp.zeros(ye_ref.shape, jnp.uint32)


def _ffn_call(totals, xe, p):
    n_exp, cap_rows, half = xe.shape
    n_blk = cap_rows // FFN_ROWS

    def xe_index(e, b, totals):
        last = jnp.maximum((totals[e] + FFN_ROWS - 1) // FFN_ROWS - 1, 0)
        return (e, jnp.minimum(b, last), 0)

    grid_spec = pltpu.PrefetchScalarGridSpec(
        num_scalar_prefetch=1,
        grid=(n_exp, n_blk),
        in_specs=[
            pl.BlockSpec((1, FFN_ROWS, half), xe_index),
            pl.BlockSpec((1, D_MODEL, FFN_HIDDEN), lambda e, b, totals: (e, 0, 0)),
            pl.BlockSpec((1, D_MODEL, FFN_HIDDEN), lambda e, b, totals: (e, 0, 0)),
            pl.BlockSpec((1, FFN_HIDDEN, D_MODEL), lambda e, b, totals: (e, 0, 0)),
        ],
        out_specs=pl.BlockSpec((1, FFN_ROWS, half), lambda e, b, totals: (e, b, 0)),
        scratch_shapes=[
            pltpu.VMEM((D_MODEL, FFN_HIDDEN), BF16),
            pltpu.VMEM((D_MODEL, FFN_HIDDEN), BF16),
            pltpu.VMEM((FFN_HIDDEN, D_MODEL), BF16),
        ],
    )
    return pl.pallas_call(
        _ffn_kernel,
        grid_spec=grid_spec,
        out_shape=jax.ShapeDtypeStruct(xe.shape, jnp.uint32),
        compiler_params=pltpu.CompilerParams(
            dimension_semantics=("arbitrary", "arbitrary"), vmem_limit_bytes=VMEM_LIMIT),
        name="ffn",
    )(totals, xe, p["w_gate"], p["w_up"], p["w_down"])


def _combine_kernel(rows_ref, offs_ref, g_ref, x1_ref, g2_ref, b2_ref, ye_ref, o_ref, land_ref, moe_ref, sem):
    i = pl.program_id(0)
    n_tiles = pl.num_programs(0)
    slot = i % 2
    n_rows = N_EXPERTS * MOE_WIN
    half = D_MODEL // 2
    fetch = lambda tile, p, s, needed: _window_copies(ye_ref, rows_ref, offs_ref, tile, p, land_ref.at[s],
                                                      sem.at[s], False, needed)

    @pl.when(i == 0)
    def _():
        _start_all(fetch(0, 0, 0, False))

    @pl.when(i + 1 < n_tiles)
    def _():
        _start_all(fetch(jnp.minimum(i + 1, n_tiles - 1), 0, 1 - slot, False))

    gates = g_ref[...]
    sel = jnp.where(gates > 0.0, 1.0, 0.0).astype(BF16)
    earlier = (lax.broadcasted_iota(jnp.int32, (MOE_TILE, MOE_TILE), 1)
               < lax.broadcasted_iota(jnp.int32, (MOE_TILE, MOE_TILE), 0))
    rank = jnp.dot(jnp.where(earlier, 1.0, 0.0).astype(BF16), sel, preferred_element_type=F32)
    spread = jnp.where(lax.broadcasted_iota(jnp.int32, (N_EXPERTS, n_rows), 1) // MOE_WIN
                       == lax.broadcasted_iota(jnp.int32, (N_EXPERTS, n_rows), 0), 1.0, 0.0).astype(BF16)
    rank_cols = jnp.dot(rank.astype(BF16), spread, preferred_element_type=F32)
    gate_cols = jnp.dot(gates.astype(BF16), spread, preferred_element_type=F32)
    slot_in_win = lax.broadcasted_iota(jnp.int32, (1, n_rows), 1) % MOE_WIN
    row_expert = lax.broadcasted_iota(jnp.int32, (n_rows, 1), 0) // MOE_WIN
    row_in_win = lax.broadcasted_iota(jnp.int32, (n_rows, 1), 0) % MOE_WIN
    seg_rows = jnp.zeros((n_rows, 1), jnp.int32)
    for e in range(N_EXPERTS):
        seg_rows = jnp.where(row_expert == e, rows_ref[e, i], seg_rows)

    def gather(p, s):
        want = (slot_in_win + p * MOE_WIN).astype(F32)
        weights = jnp.where(rank_cols == want, gate_cols, 0.0).astype(BF16)
        landed = jnp.where(row_in_win + p * MOE_WIN < seg_rows, land_ref[s], jnp.uint32(0))
        ya, yb = _unpack_bf16_pair(landed)
        return (jnp.dot(weights, ya, preferred_element_type=F32),
                jnp.dot(weights, yb, preferred_element_type=F32))

    _wait_all(fetch(i, 0, slot, False))
    ma, mb = gather(0, slot)
    moe_ref[:, :half] = ma
    moe_ref[:, half:] = mb
    n_pass = (_scalar_max(rows_ref, i) + MOE_WIN - 1) // MOE_WIN

    @pl.when(n_pass > 1)
    def _():
        def extra(p, carry):
            _start_all(fetch(i, p, 2, True))
            _wait_all(fetch(i, p, 2, True))
            ea, eb = gather(p, 2)
            moe_ref[:, :half] += ea
            moe_ref[:, half:] += eb
            return carry
        lax.fori_loop(1, n_pass, extra, 0)

    o_ref[...] = _layer_norm(DEEPNORM_ALPHA * x1_ref[...] + moe_ref[...], g2_ref[...], b2_ref[...])


def _combine_call(rows, offs, gates, x1, ye, p):
    t, n_exp = gates.shape
    n_tiles = t // MOE_TILE
    full = lambda shape: pl.BlockSpec(shape, lambda i, rows, offs: (0,) * len(shape))
    grid_spec = pltpu.PrefetchScalarGridSpec(
        num_scalar_prefetch=2,
        grid=(n_tiles,),
        in_specs=[
            pl.BlockSpec((MOE_TILE, n_exp), lambda i, rows, offs: (i, 0)),
            pl.BlockSpec((MOE_TILE, D_MODEL), lambda i, rows, offs: (i, 0)),
            full((1, D_MODEL)), full((1, D_MODEL)),
            pl.BlockSpec(memory_space=pl.ANY),
        ],
        out_specs=pl.BlockSpec((MOE_TILE, D_MODEL), lambda i, rows, offs: (i, 0)),
        scratch_shapes=[
            pltpu.VMEM((3, n_exp * MOE_WIN, D_MODEL // 2), jnp.uint32),
            pltpu.VMEM((MOE_TILE, D_MODEL), F32),
            pltpu.SemaphoreType.DMA((3,)),
        ],
    )
    return pl.pallas_call(
        _combine_kernel,
        grid_spec=grid_spec,
        out_shape=jax.ShapeDtypeStruct((t, D_MODEL), F32),
        compiler_params=pltpu.CompilerParams(
            dimension_semantics=("arbitrary",), vmem_limit_bytes=VMEM_LIMIT),
        name="combine",
    )(rows, offs, gates, x1, p["ln2_g"], p["ln2_b"], ye)


def _rope_tables(s):
    half = QK_ROPE_DIM // 2
    inv = 1.0 / (ROPE_THETA ** (jnp.arange(0, QK_ROPE_DIM, 2, dtype=F32) / QK_ROPE_DIM))
    ang = jnp.arange(s, dtype=F32)[:, None] * inv[None, :]
    cos, sin = jnp.cos(ang), jnp.sin(ang)
    tail = HEAD_PAD - QK_NOPE_DIM - QK_ROPE_DIM
    cos_t = jnp.concatenate([jnp.ones((s, QK_NOPE_DIM), F32), cos, cos, jnp.ones((s, tail), F32)], axis=1)
    sin_t = jnp.concatenate([jnp.zeros((s, QK_NOPE_DIM), F32), -sin, sin, jnp.zeros((s, tail), F32)], axis=1)
    assert half * 2 == QK_ROPE_DIM
    return cos_t, sin_t


def _prepare_params(seq_lens, ln_in_g, ln_in_b, w_in, w_pool, pool_scale, q_norm_g, w_uq, kv_norm_g, w_ukv,
                    w_o, ln1_g, ln1_b, w_router, w_gate, w_up, w_down, ln2_g, ln2_b):
    row = lambda a: a.reshape(1, -1).astype(F32)
    o3 = POOL_DIM + Q_LORA_RANK + KV_LORA_RANK
    w_in0 = w_in[0]
    w_in_p = jnp.concatenate([
        w_in0[:, :o3], jnp.zeros((D_MODEL, QK_NOPE_DIM), F32), w_in0[:, o3:],
        jnp.zeros((D_MODEL, HEAD_PAD - QK_NOPE_DIM - QK_ROPE_DIM), F32)], axis=1)
    qk = QK_NOPE_DIM + QK_ROPE_DIM
    w_uq_p = jnp.pad(w_uq[0].reshape(Q_LORA_RANK, N_HEADS, qk), ((0, 0), (0, 0), (0, HEAD_PAD - qk)))
    w_ukv0 = w_ukv[0].reshape(KV_LORA_RANK, N_HEADS, QK_NOPE_DIM + V_HEAD_DIM)
    w_k = jnp.pad(w_ukv0[:, :, :QK_NOPE_DIM], ((0, 0), (0, 0), (0, HEAD_PAD - QK_NOPE_DIM)))
    w_v = jnp.pad(w_ukv0[:, :, QK_NOPE_DIM:], ((0, 0), (0, 0), (0, V_ROWS - V_HEAD_DIM)))
    tables = {s: _rope_tables(s) for s in seq_lens}
    return {
        "ln_in_g": row(ln_in_g), "ln_in_b": row(ln_in_b),
        "w_in": w_in_p.astype(BF16),
        "w_pool": w_pool[0].astype(BF16), "pool_scale": row(pool_scale[0]),
        "q_norm_g": row(q_norm_g[0]), "w_uq": w_uq_p.reshape(Q_LORA_RANK, N_HEADS * HEAD_PAD).astype(BF16),
        "kv_norm_g": row(kv_norm_g[0]),
        "w_k": w_k.reshape(KV_LORA_RANK, N_HEADS * HEAD_PAD).astype(BF16),
        "w_vt": w_v.reshape(KV_LORA_RANK, N_HEADS * V_ROWS).T.astype(BF16),
        "w_o": w_o[0].astype(BF16),
        "ln1_g": row(ln1_g[0]), "ln1_b": row(ln1_b[0]),
        "w_router": jnp.stack([w_router[0].astype(BF16),
                               (w_router[0] - w_router[0].astype(BF16).astype(F32)).astype(BF16)]),
        "w_gate": w_gate[0], "w_up": w_up[0], "w_down": w_down[0],
        "ln2_g": row(ln2_g[0]), "ln2_b": row(ln2_b[0]),
        "cos": {s: t[0] for s, t in tables.items()},
        "sin": {s: t[1] for s, t in tables.items()},
    }


def _trunk(x, p):
    b, s, d = x.shape
    t = b * s
    u, q, k, vt = _proj_call(x, p, tm=1024)
    attn = _attn_call(q, k, vt, tq=512, tk=512)
    x1, x1b, aff = _mix_call(x, u, attn, p, tm=1024)
    cap = max(1, (CAPACITY_FACTOR * t) // N_EXPERTS)
    gates_t, rows, offs = _select_call(aff.reshape(t, N_EXPERTS).T, cap)
    cap_rows = -(-(cap + ROW_ALIGN * (t // MOE_TILE) + MOE_WIN) // FFN_ROWS) * FFN_ROWS
    xe = _dispatch_call(rows, offs, gates_t, x1b.reshape(t, d), cap_rows)
    totals = offs[:, -1] + rows[:, -1]
    ye = _ffn_call(totals, xe, p)
    y = _combine_call(rows, offs, gates_t.T, x1.reshape(t, d), ye, p)
    return y.reshape(b, s, d)


def kernel(x_prompt, x_sample, ln_in_g, ln_in_b, w_in, w_pool, pool_scale, q_norm_g, w_uq, kv_norm_g, w_ukv,
           w_o, ln1_g, ln1_b, w_router, w_gate, w_up, w_down, ln2_g, ln2_b):
    assert w_in.shape[0] == DEPTH == 1
    p = _prepare_params((x_prompt.shape[1], x_sample.shape[1]), ln_in_g, ln_in_b, w_in, w_pool, pool_scale,
                        q_norm_g, w_uq, kv_norm_g, w_ukv, w_o, ln1_g, ln1_b, w_router, w_gate, w_up, w_down,
                        ln2_g, ln2_b)
    return (_trunk(x_prompt, p), _trunk(x_sample, p))
```
